```python
import math
import jax, jax.numpy as jnp
from jax import lax
import numpy as np

D_MODEL = 1024
BATCH = 16
SEQ = 2048
DEPTH = 4

N_A_LAYERS = DEPTH // 2
N_B_LAYERS = DEPTH - N_A_LAYERS
RET_HEADS = 4
RET_QK_DIM = D_MODEL // RET_HEADS
RET_V_DIM = 2 * RET_QK_DIM
RET_CHUNK = 128
RET_THETA = 10000.0
RET_IN_WIDTH = 2 * RET_HEADS * RET_QK_DIM + 2 * RET_HEADS * RET_V_DIM
DIFF_HEADS = 8
DIFF_HEAD_DIM = D_MODEL // (2 * DIFF_HEADS)
DIFF_V_DIM = 2 * DIFF_HEAD_DIM
ROPE_THETA = 500000.0
ROPE_DIM = DIFF_HEAD_DIM // 4
Q_BLOCK = 128
PEER_HEADS = 8
PEER_N_KEYS = 128
PEER_N_EXPERTS = PEER_N_KEYS * PEER_N_KEYS
PEER_KEY_DIM = 256
PEER_TOPK = 16
PEER_CHUNK = 128
DN_ALPHA = (2 * DEPTH) ** 0.25
DN_BETA = (8 * DEPTH) ** -0.25
LN_EPS = 1e-5

kernel_name = "yoco_retnet_diffattn_peer"


def layer_norm(x, g, b):
    xf = x.astype(jnp.float32)
    mu = jnp.mean(xf, axis=-1, keepdims=True)
    var = jnp.mean(jnp.square(xf - mu), axis=-1, keepdims=True)
    y = (xf - mu) * lax.rsqrt(var + LN_EPS)
    return (y * g.astype(jnp.float32) + b.astype(jnp.float32)).astype(x.dtype)


def rope(x, angles):
    r2 = angles.shape[-1]
    r = 2 * r2
    shape = (1, angles.shape[0]) + (1,) * (x.ndim - 3) + (r2,)
    cos = jnp.cos(angles).reshape(shape)
    sin = jnp.sin(angles).reshape(shape)
    xf = x.astype(jnp.float32)
    x1 = xf[..., :r2]
    x2 = xf[..., r2:r]
    out = jnp.concatenate([x1 * cos - x2 * sin, x1 * sin + x2 * cos, xf[..., r:]], axis=-1)
    return out.astype(x.dtype)


def retention(x, w_in, w_out, angles):
    b, s, _ = x.shape
    nc = s // RET_CHUNK
    hq = RET_HEADS * RET_QK_DIM
    hv = RET_HEADS * RET_V_DIM
    proj = x @ w_in
    q = proj[..., :hq].reshape(b, s, RET_HEADS, RET_QK_DIM)
    k = proj[..., hq:2 * hq].reshape(b, s, RET_HEADS, RET_QK_DIM)
    v = proj[..., 2 * hq:2 * hq + hv].reshape(b, s, RET_HEADS, RET_V_DIM)
    g = proj[..., 2 * hq + hv:]
    q = rope(q, angles)
    k = rope(k, angles) * (RET_QK_DIM ** -0.5)

    def to_chunks(t):
        return t.astype(jnp.float32).reshape(b, nc, RET_CHUNK, RET_HEADS, -1).transpose(1, 0, 3, 2, 4)

    qc, kc, vc = to_chunks(q), to_chunks(k), to_chunks(v)
    log_g = jnp.log(1.0 - jnp.exp2(-5.0 - jnp.arange(RET_HEADS, dtype=jnp.float32)))
    ar = jnp.arange(RET_CHUNK, dtype=jnp.float32)
    rel = ar[:, None] - ar[None, :]
    decay_mask = jnp.where(rel[None] >= 0, jnp.exp(jnp.maximum(rel, 0.0)[None] * log_g[:, None, None]), 0.0)
    q_decay = jnp.exp((ar + 1.0)[None] * log_g[:, None])[None, :, :, None]
    k_decay = jnp.exp((RET_CHUNK - 1.0 - ar)[None] * log_g[:, None])[None, :, :, None]
    chunk_decay = jnp.exp(RET_CHUNK * log_g)[None, :, None, None]

    def step(state, inp):
        qi, ki, vi = inp
        sc = jnp.einsum('bhid,bhjd->bhij', qi, ki) * decay_mask
        intra = jnp.einsum('bhij,bhje->bhie', sc, vi)
        inter = jnp.einsum('bhid,bhde->bhie', qi, state) * q_decay
        state = state * chunk_decay + jnp.einsum('bhjd,bhje->bhde', ki * k_decay, vi)
        return state, intra + inter

    state0 = jnp.zeros((b, RET_HEADS, RET_QK_DIM, RET_V_DIM), jnp.float32)
    _, y = lax.scan(step, state0, (qc, kc, vc))
    y = y.transpose(1, 0, 3, 2, 4).reshape(b, s, RET_HEADS, RET_V_DIM)
    mu = jnp.mean(y, axis=-1, keepdims=True)
    var = jnp.mean(jnp.square(y - mu), axis=-1, keepdims=True)
    y = ((y - mu) * lax.rsqrt(var + LN_EPS)).reshape(b, s, hv)
    return (jax.nn.silu(g) * y.astype(x.dtype)) @ w_out


def diff_attention(x, w_q, lam_p, subln_g, w_out, k_sh, v_sh, layer_idx, angles):
    b, s, _ = x.shape
    q = (x @ w_q).reshape(b, s, DIFF_HEADS, 2, DIFF_HEAD_DIM)
    q = rope(q, angles) * (DIFF_HEAD_DIM ** -0.5)
    lam_init = 0.8 - 0.6 * math.exp(-0.3 * layer_idx)
    lp = lam_p.astype(jnp.float32)
    lam = jnp.exp(jnp.sum(lp[0] * lp[1])) - jnp.exp(jnp.sum(lp[2] * lp[3])) + lam_init
    outs = []
    for i in range(s // Q_BLOCK):
        s0 = i * Q_BLOCK
        end = s0 + Q_BLOCK
        qb = q[:, s0:end]
        kb = k_sh[:, :end]
        vb = v_sh[:, :end]
        sc = jnp.einsum('bqhmd,bkhmd->bhmqk', qb, kb).astype(jnp.float32)
        mask = jnp.arange(end)[None, :] <= (s0 + jnp.arange(Q_BLOCK))[:, None]
        p = jax.nn.softmax(jnp.where(mask, sc, -jnp.inf), axis=-1)
        a = p[:, :, 0] - lam * p[:, :, 1]
        outs.append(jnp.einsum('bhqk,bkhe->bqhe', a.astype(vb.dtype), vb))
    o = jnp.concatenate(outs, axis=1).astype(jnp.float32)
    o = o * lax.rsqrt(jnp.mean(jnp.square(o), axis=-1, keepdims=True) + LN_EPS)
    o = o * subln_g.astype(jnp.float32) * (1.0 - lam_init)
    return o.reshape(b, s, DIFF_HEADS * DIFF_V_DIM).astype(x.dtype) @ w_out


def peer_ffn(x, w_q, subkeys, u, v):
    b, s, d = x.shape
    xs = x.reshape(b * s // PEER_CHUNK, PEER_CHUNK, d)
    half = PEER_KEY_DIM // 2
    ncand = PEER_TOPK * PEER_TOPK

    def block(xc):
        q = (xc @ w_q).reshape(PEER_CHUNK, PEER_HEADS, 2, half)
        sc = jnp.einsum('thpd,hpnd->thpn', q, subkeys).astype(jnp.float32)
        s1, i1 = lax.top_k(sc[:, :, 0], PEER_TOPK)
        s2, i2 = lax.top_k(sc[:, :, 1], PEER_TOPK)
        cand = (s1[..., :, None] + s2[..., None, :]).reshape(PEER_CHUNK, PEER_HEADS, ncand)
        cidx = (i1[..., :, None] * PEER_N_KEYS + i2[..., None, :]).reshape(PEER_CHUNK, PEER_HEADS, ncand)
        top, pos = lax.top_k(cand, PEER_TOPK)
        idx = jnp.take_along_axis(cidx, pos, axis=-1).reshape(PEER_CHUNK, PEER_HEADS * PEER_TOPK)
        gate = jax.nn.softmax(top, axis=-1).reshape(PEER_CHUNK, PEER_HEADS * PEER_TOPK)
        u_e = jnp.take(u, idx, axis=0)
        v_e = jnp.take(v, idx, axis=0)
        act = jax.nn.gelu(jnp.einsum('ted,td->te', u_e, xc).astype(jnp.float32))
        return jnp.einsum('te,ted->td', (gate * act).astype(v_e.dtype), v_e)

    return lax.map(block, xs).reshape(b, s, d).astype(x.dtype)


def setup_inputs(seed: int = 0) -> dict:
    key = jax.random.key(seed)
    ks = jax.random.split(key, 14)
    f32 = jnp.float32
    d = D_MODEL
    nrm = lambda k, shape, scale: jax.random.normal(k, shape, f32) * scale
    return {
        "x": nrm(ks[0], (BATCH, SEQ, d), 1.0),
        "ret_w_in": nrm(ks[1], (N_A_LAYERS, d, RET_IN_WIDTH), d ** -0.5),
        "ret_w_out": nrm(ks[2], (N_A_LAYERS, RET_HEADS * RET_V_DIM, d), DN_BETA * (RET_HEADS * RET_V_DIM) ** -0.5),
        "kv_w": nrm(ks[3], (d, DIFF_HEADS * 2 * DIFF_HEAD_DIM + DIFF_HEADS * DIFF_V_DIM), d ** -0.5),
        "diff_w_q": nrm(ks[4], (N_B_LAYERS, d, DIFF_HEADS * 2 * DIFF_HEAD_DIM), d ** -0.5),
        "diff_lambda": nrm(ks[5], (N_B_LAYERS, 4, DIFF_HEAD_DIM), 0.1),
        "diff_subln_g": 1.0 + nrm(ks[6], (N_B_LAYERS, DIFF_V_DIM), 0.02),
        "diff_w_out": nrm(ks[7], (N_B_LAYERS, DIFF_HEADS * DIFF_V_DIM, d), DN_BETA * (DIFF_HEADS * DIFF_V_DIM) ** -0.5),
        "peer_w_q": nrm(ks[8], (DEPTH, d, PEER_HEADS * PEER_KEY_DIM), d ** -0.5),
        "peer_subkeys": nrm(ks[9], (DEPTH, PEER_HEADS, 2, PEER_N_KEYS, PEER_KEY_DIM // 2), (PEER_KEY_DIM // 2) ** -0.5),
        "peer_u": nrm(ks[10], (DEPTH, PEER_N_EXPERTS, d), d ** -0.5),
        "peer_v": nrm(ks[11], (DEPTH, PEER_N_EXPERTS, d), DN_BETA * PEER_HEADS ** -0.5),
        "ln_g": 1.0 + nrm(ks[12], (DEPTH, 2, d), 0.02),
        "ln_b": nrm(ks[13], (DEPTH, 2, d), 0.02),
    }


def reference(x, ret_w_in, ret_w_out, kv_w, diff_w_q, diff_lambda, diff_subln_g, diff_w_out,
              peer_w_q, peer_subkeys, peer_u, peer_v, ln_g, ln_b):
    b, s, _ = x.shape
    pos = jnp.arange(s, dtype=jnp.float32)
    ret_freqs = 1.0 / (RET_THETA ** jnp.linspace(0.0, 1.0, RET_QK_DIM // 2, dtype=jnp.float32))
    ret_angles = pos[:, None] * ret_freqs[None, :]
    diff_freqs = ROPE_THETA ** (-jnp.arange(0, ROPE_DIM, 2, dtype=jnp.float32) / ROPE_DIM)
    diff_angles = pos[:, None] * diff_freqs[None, :]
    kw = DIFF_HEADS * 2 * DIFF_HEAD_DIM
    k_sh = None
    v_sh = None
    for l in range(DEPTH):
        if l < N_A_LAYERS:
            mix = retention(x, ret_w_in[l], ret_w_out[l], ret_angles)
        else:
            j = l - N_A_LAYERS
            mix = diff_attention(x, diff_w_q[j], diff_lambda[j], diff_subln_g[j], diff_w_out[j],
                                 k_sh, v_sh, l, diff_angles)
        x = layer_norm(DN_ALPHA * x + mix, ln_g[l, 0], ln_b[l, 0])
        x = layer_norm(DN_ALPHA * x + peer_ffn(x, peer_w_q[l], peer_subkeys[l], peer_u[l], peer_v[l]),
                       ln_g[l, 1], ln_b[l, 1])
        if l == N_A_LAYERS - 1:
            kv = x @ kv_w
            k_sh = rope(kv[..., :kw].reshape(b, s, DIFF_HEADS, 2, DIFF_HEAD_DIM), diff_angles)
            v_sh = kv[..., kw:].reshape(b, s, DIFF_HEADS, DIFF_V_DIM)
    return x
```

```python
import functools
import math

import jax
import jax.numpy as jnp
from jax import lax
from jax.experimental import pallas as pl
from jax.experimental.pallas import tpu as pltpu

RET_HEADS = 4
RET_CHUNK = 128
RET_THETA = 10000.0
DIFF_HEADS = 8
ROPE_THETA = 500000.0
PEER_HEADS = 8
PEER_N_KEYS = 128
PEER_TOPK = 16
LN_EPS = 1e-5

LANES = 128
VMEM_LIMIT = 56 * 1024 * 1024
NEG_BIG = -1e30

F32 = jnp.float32
BF16 = jnp.bfloat16


def _params(sem, vmem=VMEM_LIMIT):
    return pltpu.CompilerParams(dimension_semantics=sem, vmem_limit_bytes=vmem)


def _dot(a, b):
    return jnp.dot(a, b, preferred_element_type=F32)


def _dot_nt(a, b):
    return lax.dot_general(a, b, (((1,), (1,)), ((), ())), preferred_element_type=F32)


def _dot_tn(a, b):
    return lax.dot_general(a, b, (((0,), (0,)), ((), ())), preferred_element_type=F32)


def _mm_kernel(x_ref, w_ref, o_ref):
    o_ref[...] = _dot(x_ref[...], w_ref[...]).astype(o_ref.dtype)


def _matmul(x, w, *, tm, tn, out_dtype=BF16):
    t, k = x.shape
    n = w.shape[1]
    return pl.pallas_call(
        _mm_kernel,
        grid=(t // tm, n // tn),
        in_specs=[pl.BlockSpec((tm, k), lambda i, j: (i, 0)),
                  pl.BlockSpec((k, tn), lambda i, j: (0, j))],
        out_specs=pl.BlockSpec((tm, tn), lambda i, j: (i, j)),
        out_shape=jax.ShapeDtypeStruct((t, n), out_dtype),
        compiler_params=_params(("parallel", "arbitrary")),
        name="matmul",
    )(x, w)


def _mm_rope_full_kernel(x_ref, w_ref, cos_ref, sin_ref, o_ref, *, head_dim, n_unscaled_tiles, scale):
    acc = _dot(x_ref[...], w_ref[...])
    cos = cos_ref[...]
    sin = sin_ref[...]
    half = head_dim // 2
    s = jnp.where(pl.program_id(1) >= n_unscaled_tiles, scale, 1.0).astype(F32)
    for h in range(acc.shape[1] // head_dim):
        x1 = acc[:, h * head_dim:h * head_dim + half]
        x2 = acc[:, h * head_dim + half:(h + 1) * head_dim]
        o_ref[:, h * head_dim:h * head_dim + half] = ((x1 * cos - x2 * sin) * s).astype(o_ref.dtype)
        o_ref[:, h * head_dim + half:(h + 1) * head_dim] = ((x1 * sin + x2 * cos) * s).astype(o_ref.dtype)


def _matmul_rope_full(x, w, cos, sin, *, seq, head_dim, n_unscaled_cols, scale, tm, tn):
    t, k = x.shape
    n = w.shape[1]
    pos_tiles = seq // tm
    kern = functools.partial(_mm_rope_full_kernel, head_dim=head_dim,
                             n_unscaled_tiles=n_unscaled_cols // tn, scale=scale)
    return pl.pallas_call(
        kern,
        grid=(t // tm, n // tn),
        in_specs=[pl.BlockSpec((tm, k), lambda i, j: (i, 0)),
                  pl.BlockSpec((k, tn), lambda i, j: (0, j)),
                  pl.BlockSpec((tm, head_dim // 2), lambda i, j: (i % pos_tiles, 0)),
                  pl.BlockSpec((tm, head_dim // 2), lambda i, j: (i % pos_tiles, 0))],
        out_specs=pl.BlockSpec((tm, tn), lambda i, j: (i, j)),
        out_shape=jax.ShapeDtypeStruct((t, n), BF16),
        compiler_params=_params(("parallel", "arbitrary")),
        name="matmul_rope_full",
    )(x, w, cos, sin)


def _mm_rope_partial_kernel(x_ref, w_ref, c_ref, s_next_ref, s_prev_ref, o_ref, *, rot, scale):
    acc = _dot(x_ref[...], w_ref[...])
    c = c_ref[...]
    s_next = s_next_ref[...]
    s_prev = s_prev_ref[...]
    for g in range(acc.shape[1] // LANES):
        xg = acc[:, g * LANES:(g + 1) * LANES]
        nxt = pltpu.roll(xg, LANES - rot, axis=1)
        prv = pltpu.roll(xg, rot, axis=1)
        o_ref[:, g * LANES:(g + 1) * LANES] = ((xg * c + nxt * s_next + prv * s_prev) * scale).astype(o_ref.dtype)


def _matmul_rope_partial(x, w, tables, *, seq, rot, scale, tm, tn):
    t, k = x.shape
    n = w.shape[1]
    pos_tiles = seq // tm
    kern = functools.partial(_mm_rope_partial_kernel, rot=rot, scale=scale)
    tab_spec = pl.BlockSpec((tm, LANES), lambda i, j: (i % pos_tiles, 0))
    return pl.pallas_call(
        kern,
        grid=(t // tm, n // tn),
        in_specs=[pl.BlockSpec((tm, k), lambda i, j: (i, 0)),
                  pl.BlockSpec((k, tn), lambda i, j: (0, j)),
                  tab_spec, tab_spec, tab_spec],
        out_specs=pl.BlockSpec((tm, tn), lambda i, j: (i, j)),
        out_shape=jax.ShapeDtypeStruct((t, n), BF16),
        compiler_params=_params(("parallel", "arbitrary")),
        name="matmul_rope_partial",
    )(x, w, *tables)


def _layer_norm_rows(z, g, b):
    mu = jnp.mean(z, axis=-1, keepdims=True)
    zc = z - mu
    var = jnp.mean(zc * zc, axis=-1, keepdims=True)
    return zc * lax.rsqrt(var + LN_EPS) * g + b


def _mm_res_ln_kernel(y_ref, w_ref, x_ref, g_ref, b_ref, o_ref, obf_ref, *, alpha):
    mix = _dot(y_ref[...], w_ref[...])
    out = _layer_norm_rows(alpha * x_ref[...] + mix, g_ref[...], b_ref[...])
    o_ref[...] = out
    obf_ref[...] = out.astype(BF16)


def _matmul_residual_ln(y, w, x, g, b, *, alpha, tm):
    t, k = y.shape
    d = w.shape[1]
    row = pl.BlockSpec((tm, d), lambda i: (i, 0))
    vec = pl.BlockSpec((1, d), lambda i: (0, 0))
    return pl.pallas_call(
        functools.partial(_mm_res_ln_kernel, alpha=alpha),
        grid=(t // tm,),
        in_specs=[pl.BlockSpec((tm, k), lambda i: (i, 0)),
                  pl.BlockSpec((k, d), lambda i: (0, 0)),
                  row, vec, vec],
        out_specs=[row, row],
        out_shape=[jax.ShapeDtypeStruct((t, d), F32), jax.ShapeDtypeStruct((t, d), BF16)],
        compiler_params=_params(("parallel",)),
        name="matmul_residual_ln",
    )(y, w, x, g.reshape(1, d), b.reshape(1, d))


def _retention_kernel(q_ref, k_ref, v_ref, g_ref, dmask_ref, qdec_ref, kdec_ref, cdec_ref,
                      o_ref, state_ref, *, n_chunks):
    c = RET_CHUNK
    state_ref[...] = jnp.zeros_like(state_ref)
    dmask = dmask_ref[0]
    qdec = qdec_ref[0]
    kdec = kdec_ref[0]
    cdec = cdec_ref[0]

    def chunk(ci, carry):
        rows = pl.ds(pl.multiple_of(ci * c, c), c)
        qi = q_ref[rows, :]
        ki = k_ref[rows, :]
        vi = v_ref[rows, :]
        state = state_ref[...]
        sc = _dot_nt(qi, ki) * dmask
        intra = _dot(sc.astype(BF16), vi)
        inter = _dot(qi, state.astype(BF16)) * qdec
        kd = (ki.astype(F32) * kdec).astype(BF16)
        state_ref[...] = state * cdec + _dot_tn(kd, vi)
        y = intra + inter
        mu = jnp.mean(y, axis=-1, keepdims=True)
        yc = y - mu
        var = jnp.mean(yc * yc, axis=-1, keepdims=True)
        yn = yc * lax.rsqrt(var + LN_EPS)
        gate = g_ref[rows, :].astype(F32)
        gate = gate * jax.nn.sigmoid(gate)
        o_ref[rows, :] = (gate * yn).astype(o_ref.dtype)
        return carry

    lax.fori_loop(0, n_chunks, chunk, 0)


def _retention_core(qk, vg, *, batch, seq):
    h = RET_HEADS
    dk = qk.shape[1] // (2 * h)
    dv = vg.shape[1] // (2 * h)
    c = RET_CHUNK
    log_g = jnp.log(1.0 - jnp.exp2(-5.0 - jnp.arange(h, dtype=F32)))
    ar = jnp.arange(c, dtype=F32)
    rel = ar[:, None] - ar[None, :]
    dmask = jnp.where(rel[None] >= 0, jnp.exp(jnp.maximum(rel, 0.0)[None] * log_g[:, None, None]), 0.0)
    qdec = jnp.broadcast_to(jnp.exp((ar + 1.0)[None] * log_g[:, None])[:, :, None], (h, c, dv))
    kdec = jnp.broadcast_to(jnp.exp((c - 1.0 - ar)[None] * log_g[:, None])[:, :, None], (h, c, dk))
    cdec = jnp.broadcast_to(jnp.exp(c * log_g)[:, None, None], (h, 1, dv))
    head_tab = lambda shape: pl.BlockSpec((1,) + shape, lambda b, hh: (hh, 0, 0))
    return pl.pallas_call(
        functools.partial(_retention_kernel, n_chunks=seq // c),
        grid=(batch, h),
        in_specs=[pl.BlockSpec((seq, dk), lambda b, hh: (b, hh)),
                  pl.BlockSpec((seq, dk), lambda b, hh: (b, h + hh)),
                  pl.BlockSpec((seq, dv), lambda b, hh: (b, hh)),
                  pl.BlockSpec((seq, dv), lambda b, hh: (b, h + hh)),
                  head_tab((c, c)), head_tab((c, dv)), head_tab((c, dk)), head_tab((1, dv))],
        out_specs=pl.BlockSpec((seq, dv), lambda b, hh: (b, hh)),
        out_shape=jax.ShapeDtypeStruct((batch * seq, h * dv), BF16),
        scratch_shapes=[pltpu.VMEM((dk, dv), F32)],
        compiler_params=_params(("parallel", "arbitrary")),
        name="retention_core",
    )(qk, qk, vg, vg, dmask, qdec, kdec, cdec)


def _diff_attn_kernel(q_ref, k_ref, v_ref, lam_ref, g_ref, o_ref, m1_ref, l1_ref, m2_ref, l2_ref,
                      acc1_ref, acc2_ref, *, tq, lam_init):
    i = pl.program_id(2)
    dh = q_ref.shape[1] // 2
    q = q_ref[...]
    q1 = q[:, :dh]
    q2 = q[:, dh:]
    for r in (m1_ref, m2_ref):
        r[...] = jnp.full_like(r, NEG_BIG)
    for r in (l1_ref, l2_ref, acc1_ref, acc2_ref):
        r[...] = jnp.zeros_like(r)
    row = i * tq + lax.broadcasted_iota(jnp.int32, (tq, tq), 0)
    col0 = lax.broadcasted_iota(jnp.int32, (tq, tq), 1)

    def online(s, m_ref, l_ref):
        m_prev = m_ref[...]
        m_next = jnp.maximum(m_prev, jnp.max(s, axis=1, keepdims=True))
        corr = jnp.exp(m_prev - m_next)
        p = jnp.exp(s - m_next[:, :1])
        l_ref[...] = l_ref[...] * corr + jnp.sum(p, axis=1, keepdims=True)
        m_ref[...] = m_next
        return p, corr

    def kv_step(j, carry):
        rows = pl.ds(pl.multiple_of(j * tq, tq), tq)
        kj = k_ref[rows, :]
        vj = v_ref[rows, :]
        visible = (j * tq + col0) <= row
        s1 = jnp.where(visible, _dot_nt(q1, kj[:, :dh]), NEG_BIG)
        s2 = jnp.where(visible, _dot_nt(q2, kj[:, dh:]), NEG_BIG)
        p1, corr1 = online(s1, m1_ref, l1_ref)
        p2, corr2 = online(s2, m2_ref, l2_ref)
        pv = _dot(jnp.concatenate([p1, p2], axis=0).astype(BF16), vj)
        acc1_ref[...] = acc1_ref[...] * corr1 + pv[:tq]
        acc2_ref[...] = acc2_ref[...] * corr2 + pv[tq:]
        return carry

    lax.fori_loop(0, i + 1, kv_step, 0)

    lp = lam_ref[...]
    lam = (jnp.exp(jnp.sum(lp[0:1] * lp[1:2], axis=1, keepdims=True))
           - jnp.exp(jnp.sum(lp[2:3] * lp[3:4], axis=1, keepdims=True)) + lam_init)
    o = acc1_ref[...] / l1_ref[...] - lam * (acc2_ref[...] / l2_ref[...])
    o = o * lax.rsqrt(jnp.mean(o * o, axis=-1, keepdims=True) + LN_EPS)
    o_ref[...] = (o * g_ref[...] * (1.0 - lam_init)).astype(o_ref.dtype)


def _diff_attention_core(q, k, v, lam_p, subln_g, *, batch, seq, lam_init, tq):
    h = DIFF_HEADS
    dv = v.shape[1] // h
    q_tiles = seq // tq
    stat = pltpu.VMEM((tq, dv), F32)
    return pl.pallas_call(
        functools.partial(_diff_attn_kernel, tq=tq, lam_init=lam_init),
        grid=(batch, h, q_tiles),
        in_specs=[pl.BlockSpec((tq, dv), lambda b, hh, i: (b * q_tiles + i, hh)),
                  pl.BlockSpec((seq, dv), lambda b, hh, i: (b, hh)),
                  pl.BlockSpec((seq, dv), lambda b, hh, i: (b, hh)),
                  pl.BlockSpec(lam_p.shape, lambda b, hh, i: (0, 0)),
                  pl.BlockSpec((1, dv), lambda b, hh, i: (0, 0))],
        out_specs=pl.BlockSpec((tq, dv), lambda b, hh, i: (b * q_tiles + i, hh)),
        out_shape=jax.ShapeDtypeStruct(q.shape, BF16),
        scratch_shapes=[stat] * 6,
        compiler_params=_params(("parallel", "parallel", "arbitrary")),
        name="diff_attention_core",
    )(q, k, v, lam_p, subln_g.reshape(1, dv))


def _descending_top(s, count):
    tops = []
    cur = s
    for _ in range(count):
        m = jnp.max(cur, axis=0, keepdims=True)
        tops.append(m)
        cur = jnp.where(cur >= m, NEG_BIG, cur)
    return tops


def _peer_route_kernel(x_ref, wq_ref, keys_ref, xt_ref, s2_ref, e2_ref, thr_ref, coef_ref):
    k = PEER_TOPK
    x = x_ref[...]
    xt_ref[...] = x.T
    qt = _dot_nt(wq_ref[...], x).astype(BF16)
    half = keys_ref.shape[2]
    for h in range(PEER_HEADS):
        s1 = _dot(keys_ref[2 * h], qt[(2 * h) * half:(2 * h + 1) * half, :])
        s2 = _dot(keys_ref[2 * h + 1], qt[(2 * h + 1) * half:(2 * h + 2) * half, :])
        a = _descending_top(s1, k)
        b = _descending_top(s2, k)
        a_all = jnp.concatenate(a, axis=0)
        b_all = jnp.concatenate(b, axis=0)
        cands = [a[0] + b_all]
        cands += [a[i] + b_all[:k // 2] for i in range(1, k // 2)]
        cands += [a_all[k // 2:] + b[0]]
        cand = jnp.concatenate(cands, axis=0)
        tau = _descending_top(cand, k)[k - 1]
        top = a[0] + b[0]
        z = jnp.sum(jnp.where(cand >= tau, jnp.exp(cand - top), 0.0), axis=0, keepdims=True)
        s2_ref[h] = s2
        e2_ref[h] = jnp.exp(s2 - b[0])
        thr_ref[h] = tau - s1
        coef_ref[h] = jnp.exp(s1 - a[0]) / z


def _peer_route(x_bf, wq_t, keys, *, tm):
    t, d = x_bf.shape
    nk = keys.shape[1]
    heads = PEER_HEADS
    tok = pl.BlockSpec((heads, nk, tm), lambda i: (0, 0, i))
    tok_shape = jax.ShapeDtypeStruct((heads, nk, t), F32)
    return pl.pallas_call(
        _peer_route_kernel,
        grid=(t // tm,),
        in_specs=[pl.BlockSpec((tm, d), lambda i: (i, 0)),
                  pl.BlockSpec(wq_t.shape, lambda i: (0, 0)),
                  pl.BlockSpec(keys.shape, lambda i: (0, 0, 0))],
        out_specs=[pl.BlockSpec((d, tm), lambda i: (0, i)), tok, tok, tok, tok],
        out_shape=[jax.ShapeDtypeStruct((d, t), BF16), tok_shape, tok_shape, tok_shape, tok_shape],
        compiler_params=_params(("parallel",)),
        name="peer_route",
    )(x_bf, wq_t, keys)


def _gelu_tanh(x):
    inner = x * (0.7978845608028654 + 0.035677408136300125 * (x * x))
    hx = 0.5 * x
    return hx + hx * jnp.tanh(inner)


def _peer_expert_kernel(xt_ref, u_ref, vt_ref, s2_ref, e2_ref, thr_ref, coef_ref, x_ref, g_ref, b_ref,
                        o_ref, obf_ref, acc_ref, *, alpha, sub):
    j = pl.program_id(1)
    nk = s2_ref.shape[1]
    tm = xt_ref.shape[1]
    te = u_ref.shape[0]

    @pl.when(j == 0)
    def _():
        acc_ref[...] = jnp.zeros_like(acc_ref)

    xt = xt_ref[...]
    for sb in range(te // sub):
        ht = _dot(u_ref[sb * sub:(sb + 1) * sub, :], xt)
        parts = []
        for al in range(sub // nk):
            a = sb * (sub // nk) + al
            cols = []
            for lc in range(tm // LANES):
                lanes = slice(lc * LANES, (lc + 1) * LANES)
                w = jnp.zeros((nk, LANES), F32)
                for h in range(PEER_HEADS):
                    thr = thr_ref[h, a:a + 1, lanes]
                    coef = coef_ref[h, a:a + 1, lanes]
                    w = w + jnp.where(s2_ref[h, :, lanes] >= thr, e2_ref[h, :, lanes], 0.0) * coef
                cols.append((w * _gelu_tanh(ht[al * nk:(al + 1) * nk, lanes])).astype(BF16))
            parts.append(jnp.concatenate(cols, axis=1))
        pt = jnp.concatenate(parts, axis=0)
        acc_ref[...] += _dot(vt_ref[:, sb * sub:(sb + 1) * sub], pt)

    @pl.when(j == pl.num_programs(1) - 1)
    def _():
        out = _layer_norm_rows(alpha * x_ref[...] + acc_ref[...].T, g_ref[...], b_ref[...])
        o_ref[...] = out
        obf_ref[...] = out.astype(BF16)


def _peer_experts(xt, u, vt, s2, e2, thr, coef, x, g, b, *, alpha, tm, te, sub):
    d, t = xt.shape
    n_exp = u.shape[0]
    heads, nk, _ = s2.shape
    row = pl.BlockSpec((tm, d), lambda i, j: (i, 0))
    vec = pl.BlockSpec((1, d), lambda i, j: (0, 0))
    tok = pl.BlockSpec((heads, nk, tm), lambda i, j: (0, 0, i))
    blk = pl.BlockSpec((heads, te // nk, tm), lambda i, j: (0, j, i))
    return pl.pallas_call(
        functools.partial(_peer_expert_kernel, alpha=alpha, sub=sub),
        grid=(t // tm, n_exp // te),
        in_specs=[pl.BlockSpec((d, tm), lambda i, j: (0, i)),
                  pl.BlockSpec((te, d), lambda i, j: (j, 0)),
                  pl.BlockSpec((d, te), lambda i, j: (0, j)),
                  tok, tok, blk, blk, row, vec, vec],
        out_specs=[row, row],
        out_shape=[jax.ShapeDtypeStruct((t, d), F32), jax.ShapeDtypeStruct((t, d), BF16)],
        scratch_shapes=[pltpu.VMEM((d, tm), F32)],
        compiler_params=_params(("parallel", "arbitrary")),
        name="peer_experts",
    )(xt, u, vt, s2, e2, thr, coef, x, g.reshape(1, d), b.reshape(1, d))


def _tiles(batch, seq):
    t = batch * seq
    return dict(
        mm_tm=min(seq, 1024), mm_tn=512,
        ln_tm=min(t, 512),
        attn_tq=min(seq, 256),
        route_tm=min(t, 512),
        exp_tm=min(t, 512), exp_te=1024, exp_sub=256,
    )


def _rope_tables_full(seq, dim, theta):
    pos = jnp.arange(seq, dtype=F32)
    freqs = 1.0 / (theta ** jnp.linspace(0.0, 1.0, dim // 2, dtype=F32))
    ang = pos[:, None] * freqs[None, :]
    return jnp.cos(ang), jnp.sin(ang)


def _rope_tables_partial(seq, head_dim, rope_dim, theta):
    pos = jnp.arange(seq, dtype=F32)
    freqs = theta ** (-jnp.arange(0, rope_dim, 2, dtype=F32) / rope_dim)
    ang = pos[:, None] * freqs[None, :]
    r2 = rope_dim // 2
    lane = jnp.arange(LANES) % head_dim
    cos = jnp.take(jnp.cos(ang), lane % r2, axis=1)
    sin = jnp.take(jnp.sin(ang), lane % r2, axis=1)
    first = (lane < r2)[None, :]
    second = ((lane >= r2) & (lane < rope_dim))[None, :]
    c = jnp.where(first | second, cos, 1.0)
    s_next = jnp.where(first, -sin, 0.0)
    s_prev = jnp.where(second, sin, 0.0)
    return c, s_next, s_prev, r2


def kernel(x, ret_w_in, ret_w_out, kv_w, diff_w_q, diff_lambda, diff_subln_g, diff_w_out,
           peer_w_q, peer_subkeys, peer_u, peer_v, ln_g, ln_b):
    batch, seq, d = x.shape
    t = batch * seq
    depth = peer_w_q.shape[0]
    n_a = ret_w_in.shape[0]
    alpha = (2 * depth) ** 0.25
    tl = _tiles(batch, seq)

    ret_dk = d // RET_HEADS
    hq = RET_HEADS * ret_dk
    diff_dh = d // (2 * DIFF_HEADS)
    kw = DIFF_HEADS * 2 * diff_dh
    ret_cos, ret_sin = _rope_tables_full(seq, ret_dk, RET_THETA)
    dc, ds_next, ds_prev, rot = _rope_tables_partial(seq, diff_dh, diff_dh // 4, ROPE_THETA)

    xf = x.reshape(t, d)
    xb = xf.astype(BF16)
    k_sh = v_sh = None
    for l in range(depth):
        if l < n_a:
            w_in = ret_w_in[l].astype(BF16)
            qk = _matmul_rope_full(xb, w_in[:, :2 * hq], ret_cos, ret_sin, seq=seq, head_dim=ret_dk,
                                   n_unscaled_cols=hq, scale=ret_dk ** -0.5,
                                   tm=tl["mm_tm"], tn=tl["mm_tn"])
            vg = _matmul(xb, w_in[:, 2 * hq:], tm=tl["mm_tm"], tn=tl["mm_tn"])
            y = _retention_core(qk, vg, batch=batch, seq=seq)
            w_out = ret_w_out[l].astype(BF16)
        else:
            j = l - n_a
            q = _matmul_rope_partial(xb, diff_w_q[j].astype(BF16), (dc, ds_next, ds_prev), seq=seq, rot=rot,
                                     scale=diff_dh ** -0.5, tm=tl["mm_tm"], tn=tl["mm_tn"])
            lam_init = 0.8 - 0.6 * math.exp(-0.3 * l)
            y = _diff_attention_core(q, k_sh, v_sh, diff_lambda[j], diff_subln_g[j], batch=batch, seq=seq,
                                     lam_init=lam_init, tq=tl["attn_tq"])
            w_out = diff_w_out[j].astype(BF16)
        xf, xb = _matmul_residual_ln(y, w_out, xf, ln_g[l, 0], ln_b[l, 0], alpha=alpha, tm=tl["ln_tm"])

        n_keys = peer_subkeys.shape[3]
        keys = peer_subkeys[l].reshape(2 * PEER_HEADS, n_keys, -1).astype(BF16)
        xt, s2, e2, thr, coef = _peer_route(xb, peer_w_q[l].T.astype(BF16), keys, tm=tl["route_tm"])
        xf, xb = _peer_experts(xt, peer_u[l].astype(BF16), peer_v[l].T.astype(BF16), s2, e2, thr, coef,
                               xf, ln_g[l, 1], ln_b[l, 1], alpha=alpha,
                               tm=tl["exp_tm"], te=tl["exp_te"], sub=tl["exp_sub"])
        if l == n_a - 1:
            kv_wb = kv_w.astype(BF16)
            k_sh = _matmul_rope_partial(xb, kv_wb[:, :kw], (dc, ds_next, ds_prev), seq=seq, rot=rot,
                                        scale=1.0, tm=tl["mm_tm"], tn=tl["mm_tn"])
            v_sh = _matmul(xb, kv_wb[:, kw:], tm=tl["mm_tm"], tn=tl["mm_tn"])
    return xf.reshape(batch, seq, d)
```

```python
import functools
import math

import jax
import jax.numpy as jnp
from jax import lax
from jax.experimental import pallas as pl
from jax.experimental.pallas import tpu as pltpu

RET_HEADS = 4
RET_CHUNK = 128
RET_THETA = 10000.0
DIFF_HEADS = 8
ROPE_THETA = 500000.0
PEER_HEADS = 8
PEER_N_KEYS = 128
PEER_TOPK = 16
LN_EPS = 1e-5

LANES = 128
VMEM_LIMIT = 56 * 1024 * 1024
NEG_BIG = -1e30

F32 = jnp.float32
BF16 = jnp.bfloat16


def _params(sem, vmem=VMEM_LIMIT):
    return pltpu.CompilerParams(dimension_semantics=sem, vmem_limit_bytes=vmem)


def _dot(a, b):
    return jnp.dot(a, b, preferred_element_type=F32)


def _dot_nt(a, b):
    return lax.dot_general(a, b, (((1,), (1,)), ((), ())), preferred_element_type=F32)


def _dot_tn(a, b):
    return lax.dot_general(a, b, (((0,), (0,)), ((), ())), preferred_element_type=F32)


def _mm_kernel(x_ref, w_ref, o_ref):
    o_ref[...] = _dot(x_ref[...], w_ref[...]).astype(o_ref.dtype)


def _matmul(x, w, *, tm, tn, out_dtype=BF16):
    t, k = x.shape
    n = w.shape[1]
    return pl.pallas_call(
        _mm_kernel,
        grid=(t // tm, n // tn),
        in_specs=[pl.BlockSpec((tm, k), lambda i, j: (i, 0)),
                  pl.BlockSpec((k, tn), lambda i, j: (0, j))],
        out_specs=pl.BlockSpec((tm, tn), lambda i, j: (i, j)),
        out_shape=jax.ShapeDtypeStruct((t, n), out_dtype),
        compiler_params=_params(("parallel", "arbitrary")),
        name="matmul",
    )(x, w)


def _mm_rope_full_kernel(x_ref, w_ref, cos_ref, sin_ref, o_ref, *, head_dim, n_unscaled_tiles, scale):
    acc = _dot(x_ref[...], w_ref[...])
    cos = cos_ref[...]
    sin = sin_ref[...]
    half = head_dim // 2
    s = jnp.where(pl.program_id(1) >= n_unscaled_tiles, scale, 1.0).astype(F32)
    for h in range(acc.shape[1] // head_dim):
        x1 = acc[:, h * head_dim:h * head_dim + half]
        x2 = acc[:, h * head_dim + half:(h + 1) * head_dim]
        o_ref[:, h * head_dim:h * head_dim + half] = ((x1 * cos - x2 * sin) * s).astype(o_ref.dtype)
        o_ref[:, h * head_dim + half:(h + 1) * head_dim] = ((x1 * sin + x2 * cos) * s).astype(o_ref.dtype)


def _matmul_rope_full(x, w, cos, sin, *, seq, head_dim, n_unscaled_cols, scale, tm, tn):
    t, k = x.shape
    n = w.shape[1]
    pos_tiles = seq // tm
    kern = functools.partial(_mm_rope_full_kernel, head_dim=head_dim,
                             n_unscaled_tiles=n_unscaled_cols // tn, scale=scale)
    return pl.pallas_call(
        kern,
        grid=(t // tm, n // tn),
        in_specs=[pl.BlockSpec((tm, k), lambda i, j: (i, 0)),
                  pl.BlockSpec((k, tn), lambda i, j: (0, j)),
                  pl.BlockSpec((tm, head_dim // 2), lambda i, j: (i % pos_tiles, 0)),
                  pl.BlockSpec((tm, head_dim // 2), lambda i, j: (i % pos_tiles, 0))],
        out_specs=pl.BlockSpec((tm, tn), lambda i, j: (i, j)),
        out_shape=jax.ShapeDtypeStruct((t, n), BF16),
        compiler_params=_params(("parallel", "arbitrary")),
        name="matmul_rope_full",
    )(x, w, cos, sin)


def _mm_rope_partial_kernel(x_ref, w_ref, c_ref, s_next_ref, s_prev_ref, o_ref, *, rot, scale):
    acc = _dot(x_ref[...], w_ref[...])
    c = c_ref[...]
    s_next = s_next_ref[...]
    s_prev = s_prev_ref[...]
    for g in range(acc.shape[1] // LANES):
        xg = acc[:, g * LANES:(g + 1) * LANES]
        nxt = pltpu.roll(xg, LANES - rot, axis=1)
        prv = pltpu.roll(xg, rot, axis=1)
        o_ref[:, g * LANES:(g + 1) * LANES] = ((xg * c + nxt * s_next + prv * s_prev) * scale).astype(o_ref.dtype)


def _matmul_rope_partial(x, w, tables, *, seq, rot, scale, tm, tn):
    t, k = x.shape
    n = w.shape[1]
    pos_tiles = seq // tm
    kern = functools.partial(_mm_rope_partial_kernel, rot=rot, scale=scale)
    tab_spec = pl.BlockSpec((tm, LANES), lambda i, j: (i % pos_tiles, 0))
    return pl.pallas_call(
        kern,
        grid=(t // tm, n // tn),
        in_specs=[pl.BlockSpec((tm, k), lambda i, j: (i, 0)),
                  pl.BlockSpec((k, tn), lambda i, j: (0, j)),
                  tab_spec, tab_spec, tab_spec],
        out_specs=pl.BlockSpec((tm, tn), lambda i, j: (i, j)),
        out_shape=jax.ShapeDtypeStruct((t, n), BF16),
        compiler_params=_params(("parallel", "arbitrary")),
        name="matmul_rope_partial",
    )(x, w, *tables)


def _layer_norm_rows(z, g, b):
    mu = jnp.mean(z, axis=-1, keepdims=True)
    zc = z - mu
    var = jnp.mean(zc * zc, axis=-1, keepdims=True)
    return zc * lax.rsqrt(var + LN_EPS) * g + b


def _mm_res_ln_kernel(y_ref, w_ref, x_ref, g_ref, b_ref, o_ref, obf_ref, *, alpha):
    mix = _dot(y_ref[...], w_ref[...])
    out = _layer_norm_rows(alpha * x_ref[...] + mix, g_ref[...], b_ref[...])
    o_ref[...] = out
    obf_ref[...] = out.astype(BF16)


def _matmul_residual_ln(y, w, x, g, b, *, alpha, tm):
    t, k = y.shape
    d = w.shape[1]
    row = pl.BlockSpec((tm, d), lambda i: (i, 0))
    vec = pl.BlockSpec((1, d), lambda i: (0, 0))
    return pl.pallas_call(
        functools.partial(_mm_res_ln_kernel, alpha=alpha),
        grid=(t // tm,),
        in_specs=[pl.BlockSpec((tm, k), lambda i: (i, 0)),
                  pl.BlockSpec((k, d), lambda i: (0, 0)),
                  row, vec, vec],
        out_specs=[row, row],
        out_shape=[jax.ShapeDtypeStruct((t, d), F32), jax.ShapeDtypeStruct((t, d), BF16)],
        compiler_params=_params(("parallel",)),
        name="matmul_residual_ln",
    )(y, w, x, g.reshape(1, d), b.reshape(1, d))


def _retention_kernel(q_ref, k_ref, v_ref, g_ref, dmask_ref, qdec_ref, kdec_ref, cdec_ref,
                      o_ref, state_ref, *, n_chunks):
    c = RET_CHUNK
    state_ref[...] = jnp.zeros_like(state_ref)
    dmask = dmask_ref[0]
    qdec = qdec_ref[0]
    kdec = kdec_ref[0]
    cdec = cdec_ref[0]

    def chunk(ci, carry):
        rows = pl.ds(pl.multiple_of(ci * c, c), c)
        qi = q_ref[rows, :]
        ki = k_ref[rows, :]
        vi = v_ref[rows, :]
        state = state_ref[...]
        sc = _dot_nt(qi, ki) * dmask
        intra = _dot(sc.astype(BF16), vi)
        inter = _dot(qi, state.astype(BF16)) * qdec
        kd = (ki.astype(F32) * kdec).astype(BF16)
        state_ref[...] = state * cdec + _dot_tn(kd, vi)
        y = intra + inter
        mu = jnp.mean(y, axis=-1, keepdims=True)
        yc = y - mu
        var = jnp.mean(yc * yc, axis=-1, keepdims=True)
        yn = yc * lax.rsqrt(var + LN_EPS)
        gate = g_ref[rows, :].astype(F32)
        gate = gate * jax.nn.sigmoid(gate)
        o_ref[rows, :] = (gate * yn).astype(o_ref.dtype)
        return carry

    lax.fori_loop(0, n_chunks, chunk, 0)


def _retention_core(qk, vg, *, batch, seq):
    h = RET_HEADS
    dk = qk.shape[1] // (2 * h)
    dv = vg.shape[1] // (2 * h)
    c = RET_CHUNK
    log_g = jnp.log(1.0 - jnp.exp2(-5.0 - jnp.arange(h, dtype=F32)))
    ar = jnp.arange(c, dtype=F32)
    rel = ar[:, None] - ar[None, :]
    dmask = jnp.where(rel[None] >= 0, jnp.exp(jnp.maximum(rel, 0.0)[None] * log_g[:, None, None]), 0.0)
    qdec = jnp.broadcast_to(jnp.exp((ar + 1.0)[None] * log_g[:, None])[:, :, None], (h, c, dv))
    kdec = jnp.broadcast_to(jnp.exp((c - 1.0 - ar)[None] * log_g[:, None])[:, :, None], (h, c, dk))
    cdec = jnp.broadcast_to(jnp.exp(c * log_g)[:, None, None], (h, 1, dv))
    head_tab = lambda shape: pl.BlockSpec((1,) + shape, lambda b, hh: (hh, 0, 0))
    return pl.pallas_call(
        functools.partial(_retention_kernel, n_chunks=seq // c),
        grid=(batch, h),
        in_specs=[pl.BlockSpec((seq, dk), lambda b, hh: (b, hh)),
                  pl.BlockSpec((seq, dk), lambda b, hh: (b, h + hh)),
                  pl.BlockSpec((seq, dv), lambda b, hh: (b, hh)),
                  pl.BlockSpec((seq, dv), lambda b, hh: (b, h + hh)),
                  head_tab((c, c)), head_tab((c, dv)), head_tab((c, dk)), head_tab((1, dv))],
        out_specs=pl.BlockSpec((seq, dv), lambda b, hh: (b, hh)),
        out_shape=jax.ShapeDtypeStruct((batch * seq, h * dv), BF16),
        scratch_shapes=[pltpu.VMEM((dk, dv), F32)],
        compiler_params=_params(("parallel", "arbitrary")),
        name="retention_core",
    )(qk, qk, vg, vg, dmask, qdec, kdec, cdec)


def _diff_attn_kernel(q_ref, k_ref, v_ref, lam_ref, g_ref, o_ref, m1_ref, l1_ref, m2_ref, l2_ref,
                      acc1_ref, acc2_ref, *, tq, lam_init):
    i = pl.program_id(2)
    dh = q_ref.shape[1] // 2
    q = q_ref[...]
    q1 = q[:, :dh]
    q2 = q[:, dh:]
    for r in (m1_ref, m2_ref):
        r[...] = jnp.full_like(r, NEG_BIG)
    for r in (l1_ref, l2_ref, acc1_ref, acc2_ref):
        r[...] = jnp.zeros_like(r)
    row = i * tq + lax.broadcasted_iota(jnp.int32, (tq, tq), 0)
    col0 = lax.broadcasted_iota(jnp.int32, (tq, tq), 1)

    def online(s, m_ref, l_ref):
        m_prev = m_ref[...]
        m_next = jnp.maximum(m_prev, jnp.max(s, axis=1, keepdims=True))
        corr = jnp.exp(m_prev - m_next)
        p = jnp.exp(s - m_next[:, :1])
        l_ref[...] = l_ref[...] * corr + jnp.sum(p, axis=1, keepdims=True)
        m_ref[...] = m_next
        return p, corr

    def kv_step(j, carry):
        rows = pl.ds(pl.multiple_of(j * tq, tq), tq)
        kj = k_ref[rows, :]
        vj = v_ref[rows, :]
        visible = (j * tq + col0) <= row
        s1 = jnp.where(visible, _dot_nt(q1, kj[:, :dh]), NEG_BIG)
        s2 = jnp.where(visible, _dot_nt(q2, kj[:, dh:]), NEG_BIG)
        p1, corr1 = online(s1, m1_ref, l1_ref)
        p2, corr2 = online(s2, m2_ref, l2_ref)
        pv = _dot(jnp.concatenate([p1, p2], axis=0).astype(BF16), vj)
        acc1_ref[...] = acc1_ref[...] * corr1 + pv[:tq]
        acc2_ref[...] = acc2_ref[...] * corr2 + pv[tq:]
        return carry

    lax.fori_loop(0, i + 1, kv_step, 0)

    lp = lam_ref[...]
    lam = (jnp.exp(jnp.sum(lp[0:1] * lp[1:2], axis=1, keepdims=True))
           - jnp.exp(jnp.sum(lp[2:3] * lp[3:4], axis=1, keepdims=True)) + lam_init)
    o = acc1_ref[...] / l1_ref[...] - lam * (acc2_ref[...] / l2_ref[...])
    o = o * lax.rsqrt(jnp.mean(o * o, axis=-1, keepdims=True) + LN_EPS)
    o_ref[...] = (o * g_ref[...] * (1.0 - lam_init)).astype(o_ref.dtype)


def _diff_attention_core(q, k, v, lam_p, subln_g, *, batch, seq, lam_init, tq):
    h = DIFF_HEADS
    dv = v.shape[1] // h
    q_tiles = seq // tq
    stat = pltpu.VMEM((tq, dv), F32)
    return pl.pallas_call(
        functools.partial(_diff_attn_kernel, tq=tq, lam_init=lam_init),
        grid=(batch, h, q_tiles),
        in_specs=[pl.BlockSpec((tq, dv), lambda b, hh, i: (b * q_tiles + i, hh)),
                  pl.BlockSpec((seq, dv), lambda b, hh, i: (b, hh)),
                  pl.BlockSpec((seq, dv), lambda b, hh, i: (b, hh)),
                  pl.BlockSpec(lam_p.shape, lambda b, hh, i: (0, 0)),
                  pl.BlockSpec((1, dv), lambda b, hh, i: (0, 0))],
        out_specs=pl.BlockSpec((tq, dv), lambda b, hh, i: (b * q_tiles + i, hh)),
        out_shape=jax.ShapeDtypeStruct(q.shape, BF16),
        scratch_shapes=[stat] * 6,
        compiler_params=_params(("parallel", "parallel", "arbitrary")),
        name="diff_attention_core",
    )(q, k, v, lam_p, subln_g.reshape(1, dv))


def _descending_top(s, count):
    tops = []
    cur = s
    rank = jnp.full(s.shape, float(count), F32)
    for r in range(count):
        m = jnp.max(cur, axis=0, keepdims=True)
        tops.append(m)
        hit = cur >= m
        rank = jnp.where(hit, float(r), rank)
        cur = jnp.where(hit, NEG_BIG, cur)
    return tops, rank


def _peer_route_kernel(x_ref, wq_ref, keys_ref, xt_ref, r2_ref, e2_ref, n_ref, coef_ref):
    k = PEER_TOPK
    x = x_ref[...]
    xt_ref[...] = x.T
    qt = _dot_nt(wq_ref[...], x).astype(BF16)
    half = keys_ref.shape[2]
    for h in range(PEER_HEADS):
        s1 = _dot(keys_ref[2 * h], qt[(2 * h) * half:(2 * h + 1) * half, :])
        s2 = _dot(keys_ref[2 * h + 1], qt[(2 * h + 1) * half:(2 * h + 2) * half, :])
        a, _ = _descending_top(s1, k)
        b, rank2 = _descending_top(s2, k)
        a_all = jnp.concatenate(a, axis=0)
        b_all = jnp.concatenate(b, axis=0)
        cands = [a[0] + b_all]
        cands += [a[i] + b_all[:k // 2] for i in range(1, k // 2)]
        cands += [a_all[k // 2:] + b[0]]
        cand = jnp.concatenate(cands, axis=0)
        tau = _descending_top(cand, k)[0][k - 1]
        top = a[0] + b[0]
        z = jnp.sum(jnp.where(cand >= tau, jnp.exp(cand - top), 0.0), axis=0, keepdims=True)
        thr = tau - s1
        n = jnp.zeros_like(s1)
        for jj in range(k):
            n = jnp.where(b[jj] >= thr, float(jj + 1), n)
        r2_ref[h] = rank2
        e2_ref[h] = jnp.exp(s2 - b[0])
        n_ref[h] = n
        coef_ref[h] = jnp.exp(s1 - a[0]) / z


def _peer_route(x_bf, wq_t, keys, *, tm):
    t, d = x_bf.shape
    nk = keys.shape[1]
    heads = PEER_HEADS
    tok = pl.BlockSpec((heads, nk, tm), lambda i: (0, 0, i))
    tok_f32 = jax.ShapeDtypeStruct((heads, nk, t), F32)
    return pl.pallas_call(
        _peer_route_kernel,
        grid=(t // tm,),
        in_specs=[pl.BlockSpec((tm, d), lambda i: (i, 0)),
                  pl.BlockSpec(wq_t.shape, lambda i: (0, 0)),
                  pl.BlockSpec(keys.shape, lambda i: (0, 0, 0))],
        out_specs=[pl.BlockSpec((d, tm), lambda i: (0, i)), tok, tok, tok, tok],
        out_shape=[jax.ShapeDtypeStruct((d, t), BF16), tok_f32, tok_f32, tok_f32, tok_f32],
        compiler_params=_params(("parallel",)),
        name="peer_route",
    )(x_bf, wq_t, keys)


def _gelu_tanh(x):
    inner = x * (0.7978845608028654 + 0.035677408136300125 * (x * x))
    hx = 0.5 * x
    return hx + hx * jnp.tanh(inner)


BF16_ROWS = 16


def _peer_expert_kernel(xt_ref, u_ref, vt_ref, r2f_ref, e2f_ref, n_ref, coef_ref, x_ref, g_ref, b_ref,
                        o_ref, obf_ref, acc_ref, r2_ref, e2_ref, *sub_refs, alpha, sub):
    j = pl.program_id(1)
    nk = r2_ref.shape[1]
    groups = nk // BF16_ROWS
    tm = xt_ref.shape[1]
    te = u_ref.shape[0]
    n_sub = te // sub
    ht_refs = sub_refs[:n_sub]
    pt_refs = sub_refs[n_sub:]

    @pl.when(j == 0)
    def _():
        acc_ref[...] = jnp.zeros_like(acc_ref)
        r2_ref[...] = r2f_ref[...].astype(BF16)
        e2_ref[...] = e2f_ref[...].astype(BF16)

    def first_matmul(sb):
        ht_refs[sb][...] = _dot(u_ref[sb * sub:(sb + 1) * sub, :], xt_ref[...])

    first_matmul(0)
    for sb in range(n_sub):
        blk = slice(sb * sub, (sb + 1) * sub)
        ht_ref = ht_refs[sb]
        pt_ref = pt_refs[sb]
        if sb + 1 < n_sub:
            first_matmul(sb + 1)
        for al in range(sub // nk):
            a = sb * (sub // nk) + al
            rows = slice(al * nk, (al + 1) * nk)
            for lc in range(tm // LANES):
                lanes = slice(lc * LANES, (lc + 1) * LANES)
                w = jnp.zeros((nk, LANES), BF16)
                for h in range(PEER_HEADS):
                    nb = jnp.broadcast_to(n_ref[h, a:a + 1, lanes], (BF16_ROWS, LANES)).astype(BF16)
                    cb = jnp.broadcast_to(coef_ref[h, a:a + 1, lanes], (BF16_ROWS, LANES)).astype(BF16)
                    nb = jnp.concatenate([nb] * groups, axis=0)
                    cb = jnp.concatenate([cb] * groups, axis=0)
                    w = w + jnp.where(r2_ref[h, :, lanes] < nb, e2_ref[h, :, lanes], 0.0) * cb
                act = _gelu_tanh(ht_ref[rows, lanes]).astype(BF16)
                pt_ref[rows, lanes] = w * act
        acc_ref[...] += _dot(vt_ref[:, blk], pt_ref[...])

    @pl.when(j == pl.num_programs(1) - 1)
    def _():
        out = _layer_norm_rows(alpha * x_ref[...] + acc_ref[...].T, g_ref[...], b_ref[...])
        o_ref[...] = out
        obf_ref[...] = out.astype(BF16)


def _peer_experts(xt, u, vt, r2, e2, n, coef, x, g, b, *, alpha, tm, te, sub):
    d, t = xt.shape
    n_exp = u.shape[0]
    heads, nk, _ = n.shape
    row = pl.BlockSpec((tm, d), lambda i, j: (i, 0))
    vec = pl.BlockSpec((1, d), lambda i, j: (0, 0))
    tok = pl.BlockSpec((heads, nk, tm), lambda i, j: (0, 0, i))
    blk = pl.BlockSpec((heads, te // nk, tm), lambda i, j: (0, j, i))
    return pl.pallas_call(
        functools.partial(_peer_expert_kernel, alpha=alpha, sub=sub),
        grid=(t // tm, n_exp // te),
        in_specs=[pl.BlockSpec((d, tm), lambda i, j: (0, i)),
                  pl.BlockSpec((te, d), lambda i, j: (j, 0)),
                  pl.BlockSpec((d, te), lambda i, j: (0, j)),
                  tok, tok, blk, blk, row, vec, vec],
        out_specs=[row, row],
        out_shape=[jax.ShapeDtypeStruct((t, d), F32), jax.ShapeDtypeStruct((t, d), BF16)],
        scratch_shapes=([pltpu.VMEM((d, tm), F32),
                         pltpu.VMEM((heads, nk, tm), BF16), pltpu.VMEM((heads, nk, tm), BF16)]
                        + [pltpu.VMEM((sub, tm), F32)] * (te // sub)
                        + [pltpu.VMEM((sub, tm), BF16)] * (te // sub)),
        compiler_params=_params(("parallel", "arbitrary")),
        name="peer_experts",
    )(xt, u, vt, r2, e2, n, coef, x, g.reshape(1, d), b.reshape(1, d))


def _tiles(batch, seq):
    t = batch * seq
    return dict(
        mm_tm=min(seq, 1024), mm_tn=512,
        ln_tm=min(t, 512),
        attn_tq=min(seq, 256),
        route_tm=min(t, 512),
        exp_tm=min(t, 512), exp_te=1024, exp_sub=256,
    )


def _rope_tables_full(seq, dim, theta):
    pos = jnp.arange(seq, dtype=F32)
    freqs = 1.0 / (theta ** jnp.linspace(0.0, 1.0, dim // 2, dtype=F32))
    ang = pos[:, None] * freqs[None, :]
    return jnp.cos(ang), jnp.sin(ang)


def _rope_tables_partial(seq, head_dim, rope_dim, theta):
    pos = jnp.arange(seq, dtype=F32)
    freqs = theta ** (-jnp.arange(0, rope_dim, 2, dtype=F32) / rope_dim)
    ang = pos[:, None] * freqs[None, :]
    r2 = rope_dim // 2
    lane = jnp.arange(LANES) % head_dim
    cos = jnp.take(jnp.cos(ang), lane % r2, axis=1)
    sin = jnp.take(jnp.sin(ang), lane % r2, axis=1)
    first = (lane < r2)[None, :]
    second = ((lane >= r2) & (lane < rope_dim))[None, :]
    c = jnp.where(first | second, cos, 1.0)
    s_next = jnp.where(first, -sin, 0.0)
    s_prev = jnp.where(second, sin, 0.0)
    return c, s_next, s_prev, r2


def kernel(x, ret_w_in, ret_w_out, kv_w, diff_w_q, diff_lambda, diff_subln_g, diff_w_out,
           peer_w_q, peer_subkeys, peer_u, peer_v, ln_g, ln_b):
    batch, seq, d = x.shape
    t = batch * seq
    depth = peer_w_q.shape[0]
    n_a = ret_w_in.shape[0]
    alpha = (2 * depth) ** 0.25
    tl = _tiles(batch, seq)

    ret_dk = d // RET_HEADS
    hq = RET_HEADS * ret_dk
    diff_dh = d // (2 * DIFF_HEADS)
    kw = DIFF_HEADS * 2 * diff_dh
    ret_cos, ret_sin = _rope_tables_full(seq, ret_dk, RET_THETA)
    dc, ds_next, ds_prev, rot = _rope_tables_partial(seq, diff_dh, diff_dh // 4, ROPE_THETA)

    xf = x.reshape(t, d)
    xb = xf.astype(BF16)
    k_sh = v_sh = None
    for l in range(depth):
        if l < n_a:
            w_in = ret_w_in[l].astype(BF16)
            qk = _matmul_rope_full(xb, w_in[:, :2 * hq], ret_cos, ret_sin, seq=seq, head_dim=ret_dk,
                                   n_unscaled_cols=hq, scale=ret_dk ** -0.5,
                                   tm=tl["mm_tm"], tn=tl["mm_tn"])
            vg = _matmul(xb, w_in[:, 2 * hq:], tm=tl["mm_tm"], tn=tl["mm_tn"])
            y = _retention_core(qk, vg, batch=batch, seq=seq)
            w_out = ret_w_out[l].astype(BF16)
        else:
            j = l - n_a
            q = _matmul_rope_partial(xb, diff_w_q[j].astype(BF16), (dc, ds_next, ds_prev), seq=seq, rot=rot,
                                     scale=diff_dh ** -0.5, tm=tl["mm_tm"], tn=tl["mm_tn"])
            lam_init = 0.8 - 0.6 * math.exp(-0.3 * l)
            y = _diff_attention_core(q, k_sh, v_sh, diff_lambda[j], diff_subln_g[j], batch=batch, seq=seq,
                                     lam_init=lam_init, tq=tl["attn_tq"])
            w_out = diff_w_out[j].astype(BF16)
        xf, xb = _matmul_residual_ln(y, w_out, xf, ln_g[l, 0], ln_b[l, 0], alpha=alpha, tm=tl["ln_tm"])

        n_keys = peer_subkeys.shape[3]
        keys = peer_subkeys[l].reshape(2 * PEER_HEADS, n_keys, -1).astype(BF16)
        xt, r2, e2, n, coef = _peer_route(xb, peer_w_q[l].T.astype(BF16), keys, tm=tl["route_tm"])
        xf, xb = _peer_experts(xt, peer_u[l].astype(BF16), peer_v[l].T.astype(BF16), r2, e2, n, coef,
                               xf, ln_g[l, 1], ln_b[l, 1], alpha=alpha,
                               tm=tl["exp_tm"], te=tl["exp_te"], sub=tl["exp_sub"])
        if l == n_a - 1:
            kv_wb = kv_w.astype(BF16)
            k_sh = _matmul_rope_partial(xb, kv_wb[:, :kw], (dc, ds_next, ds_prev), seq=seq, rot=rot,
                                        scale=1.0, tm=tl["mm_tm"], tn=tl["mm_tn"])
            v_sh = _matmul(xb, kv_wb[:, kw:], tm=tl["mm_tm"], tn=tl["mm_tn"])
    return xf.reshape(batch, seq, d)
```

```python
import functools
import math

import jax
import jax.numpy as jnp
from jax import lax
from jax.experimental import pallas as pl
from jax.experimental.pallas import tpu as pltpu

RET_HEADS = 4
RET_CHUNK = 128
RET_THETA = 10000.0
DIFF_HEADS = 8
ROPE_THETA = 500000.0
PEER_HEADS = 8
PEER_N_KEYS = 128
PEER_TOPK = 16
LN_EPS = 1e-5

LANES = 128
VMEM_LIMIT = 56 * 1024 * 1024
NEG_BIG = -1e30

F32 = jnp.float32
BF16 = jnp.bfloat16


def _params(sem, vmem=VMEM_LIMIT):
    return pltpu.CompilerParams(dimension_semantics=sem, vmem_limit_bytes=vmem)


def _dot(a, b):
    return jnp.dot(a, b, preferred_element_type=F32)


def _dot_nt(a, b):
    return lax.dot_general(a, b, (((1,), (1,)), ((), ())), preferred_element_type=F32)


def _dot_tn(a, b):
    return lax.dot_general(a, b, (((0,), (0,)), ((), ())), preferred_element_type=F32)


def _mm_kernel(x_ref, w_ref, o_ref):
    o_ref[...] = _dot(x_ref[...], w_ref[...]).astype(o_ref.dtype)


def _matmul(x, w, *, tm, tn, out_dtype=BF16):
    t, k = x.shape
    n = w.shape[1]
    return pl.pallas_call(
        _mm_kernel,
        grid=(t // tm, n // tn),
        in_specs=[pl.BlockSpec((tm, k), lambda i, j: (i, 0)),
                  pl.BlockSpec((k, tn), lambda i, j: (0, j))],
        out_specs=pl.BlockSpec((tm, tn), lambda i, j: (i, j)),
        out_shape=jax.ShapeDtypeStruct((t, n), out_dtype),
        compiler_params=_params(("parallel", "arbitrary")),
        name="matmul",
    )(x, w)


def _mm_rope_full_kernel(x_ref, w_ref, cos_ref, sin_ref, o_ref, *, head_dim, n_unscaled_tiles, scale):
    acc = _dot(x_ref[...], w_ref[...])
    cos = cos_ref[...]
    sin = sin_ref[...]
    half = head_dim // 2
    s = jnp.where(pl.program_id(1) >= n_unscaled_tiles, scale, 1.0).astype(F32)
    for h in range(acc.shape[1] // head_dim):
        x1 = acc[:, h * head_dim:h * head_dim + half]
        x2 = acc[:, h * head_dim + half:(h + 1) * head_dim]
        o_ref[:, h * head_dim:h * head_dim + half] = ((x1 * cos - x2 * sin) * s).astype(o_ref.dtype)
        o_ref[:, h * head_dim + half:(h + 1) * head_dim] = ((x1 * sin + x2 * cos) * s).astype(o_ref.dtype)


def _matmul_rope_full(x, w, cos, sin, *, seq, head_dim, n_unscaled_cols, scale, tm, tn):
    t, k = x.shape
    n = w.shape[1]
    pos_tiles = seq // tm
    kern = functools.partial(_mm_rope_full_kernel, head_dim=head_dim,
                             n_unscaled_tiles=n_unscaled_cols // tn, scale=scale)
    return pl.pallas_call(
        kern,
        grid=(t // tm, n // tn),
        in_specs=[pl.BlockSpec((tm, k), lambda i, j: (i, 0)),
                  pl.BlockSpec((k, tn), lambda i, j: (0, j)),
                  pl.BlockSpec((tm, head_dim // 2), lambda i, j: (i % pos_tiles, 0)),
                  pl.BlockSpec((tm, head_dim // 2), lambda i, j: (i % pos_tiles, 0))],
        out_specs=pl.BlockSpec((tm, tn), lambda i, j: (i, j)),
        out_shape=jax.ShapeDtypeStruct((t, n), BF16),
        compiler_params=_params(("parallel", "arbitrary")),
        name="matmul_rope_full",
    )(x, w, cos, sin)


def _mm_rope_partial_kernel(x_ref, w_ref, c_ref, s_next_ref, s_prev_ref, o_ref, *, rot, scale):
    acc = _dot(x_ref[...], w_ref[...])
    c = c_ref[...]
    s_next = s_next_ref[...]
    s_prev = s_prev_ref[...]
    for g in range(acc.shape[1] // LANES):
        xg = acc[:, g * LANES:(g + 1) * LANES]
        nxt = pltpu.roll(xg, LANES - rot, axis=1)
        prv = pltpu.roll(xg, rot, axis=1)
        o_ref[:, g * LANES:(g + 1) * LANES] = ((xg * c + nxt * s_next + prv * s_prev) * scale).astype(o_ref.dtype)


def _matmul_rope_partial(x, w, tables, *, seq, rot, scale, tm, tn):
    t, k = x.shape
    n = w.shape[1]
    pos_tiles = seq // tm
    kern = functools.partial(_mm_rope_partial_kernel, rot=rot, scale=scale)
    tab_spec = pl.BlockSpec((tm, LANES), lambda i, j: (i % pos_tiles, 0))
    return pl.pallas_call(
        kern,
        grid=(t // tm, n // tn),
        in_specs=[pl.BlockSpec((tm, k), lambda i, j: (i, 0)),
                  pl.BlockSpec((k, tn), lambda i, j: (0, j)),
                  tab_spec, tab_spec, tab_spec],
        out_specs=pl.BlockSpec((tm, tn), lambda i, j: (i, j)),
        out_shape=jax.ShapeDtypeStruct((t, n), BF16),
        compiler_params=_params(("parallel", "arbitrary")),
        name="matmul_rope_partial",
    )(x, w, *tables)


def _layer_norm_rows(z, g, b):
    mu = jnp.mean(z, axis=-1, keepdims=True)
    zc = z - mu
    var = jnp.mean(zc * zc, axis=-1, keepdims=True)
    return zc * lax.rsqrt(var + LN_EPS) * g + b


def _mm_res_ln_kernel(y_ref, w_ref, x_ref, g_ref, b_ref, o_ref, obf_ref, *, alpha):
    mix = _dot(y_ref[...], w_ref[...])
    out = _layer_norm_rows(alpha * x_ref[...] + mix, g_ref[...], b_ref[...])
    o_ref[...] = out
    obf_ref[...] = out.astype(BF16)


def _matmul_residual_ln(y, w, x, g, b, *, alpha, tm):
    t, k = y.shape
    d = w.shape[1]
    row = pl.BlockSpec((tm, d), lambda i: (i, 0))
    vec = pl.BlockSpec((1, d), lambda i: (0, 0))
    return pl.pallas_call(
        functools.partial(_mm_res_ln_kernel, alpha=alpha),
        grid=(t // tm,),
        in_specs=[pl.BlockSpec((tm, k), lambda i: (i, 0)),
                  pl.BlockSpec((k, d), lambda i: (0, 0)),
                  row, vec, vec],
        out_specs=[row, row],
        out_shape=[jax.ShapeDtypeStruct((t, d), F32), jax.ShapeDtypeStruct((t, d), BF16)],
        compiler_params=_params(("parallel",)),
        name="matmul_residual_ln",
    )(y, w, x, g.reshape(1, d), b.reshape(1, d))


def _retention_kernel(q_ref, k_ref, v_ref, g_ref, dmask_ref, qdec_ref, kdec_ref, cdec_ref,
                      o_ref, state_ref, *, n_chunks):
    c = RET_CHUNK
    state_ref[...] = jnp.zeros_like(state_ref)
    dmask = dmask_ref[0]
    qdec = qdec_ref[0]
    kdec = kdec_ref[0]
    cdec = cdec_ref[0]

    def chunk(ci, carry):
        rows = pl.ds(pl.multiple_of(ci * c, c), c)
        qi = q_ref[rows, :]
        ki = k_ref[rows, :]
        vi = v_ref[rows, :]
        state = state_ref[...]
        sc = _dot_nt(qi, ki) * dmask
        intra = _dot(sc.astype(BF16), vi)
        inter = _dot(qi, state.astype(BF16)) * qdec
        kd = (ki.astype(F32) * kdec).astype(BF16)
        state_ref[...] = state * cdec + _dot_tn(kd, vi)
        y = intra + inter
        mu = jnp.mean(y, axis=-1, keepdims=True)
        yc = y - mu
        var = jnp.mean(yc * yc, axis=-1, keepdims=True)
        yn = yc * lax.rsqrt(var + LN_EPS)
        gate = g_ref[rows, :].astype(F32)
        gate = gate * jax.nn.sigmoid(gate)
        o_ref[rows, :] = (gate * yn).astype(o_ref.dtype)
        return carry

    lax.fori_loop(0, n_chunks, chunk, 0)


def _retention_core(qk, vg, *, batch, seq):
    h = RET_HEADS
    dk = qk.shape[1] // (2 * h)
    dv = vg.shape[1] // (2 * h)
    c = RET_CHUNK
    log_g = jnp.log(1.0 - jnp.exp2(-5.0 - jnp.arange(h, dtype=F32)))
    ar = jnp.arange(c, dtype=F32)
    rel = ar[:, None] - ar[None, :]
    dmask = jnp.where(rel[None] >= 0, jnp.exp(jnp.maximum(rel, 0.0)[None] * log_g[:, None, None]), 0.0)
    qdec = jnp.broadcast_to(jnp.exp((ar + 1.0)[None] * log_g[:, None])[:, :, None], (h, c, dv))
    kdec = jnp.broadcast_to(jnp.exp((c - 1.0 - ar)[None] * log_g[:, None])[:, :, None], (h, c, dk))
    cdec = jnp.broadcast_to(jnp.exp(c * log_g)[:, None, None], (h, 1, dv))
    head_tab = lambda shape: pl.BlockSpec((1,) + shape, lambda b, hh: (hh, 0, 0))
    return pl.pallas_call(
        functools.partial(_retention_kernel, n_chunks=seq // c),
        grid=(batch, h),
        in_specs=[pl.BlockSpec((seq, dk), lambda b, hh: (b, hh)),
                  pl.BlockSpec((seq, dk), lambda b, hh: (b, h + hh)),
                  pl.BlockSpec((seq, dv), lambda b, hh: (b, hh)),
                  pl.BlockSpec((seq, dv), lambda b, hh: (b, h + hh)),
                  head_tab((c, c)), head_tab((c, dv)), head_tab((c, dk)), head_tab((1, dv))],
        out_specs=pl.BlockSpec((seq, dv), lambda b, hh: (b, hh)),
        out_shape=jax.ShapeDtypeStruct((batch * seq, h * dv), BF16),
        scratch_shapes=[pltpu.VMEM((dk, dv), F32)],
        compiler_params=_params(("parallel", "arbitrary")),
        name="retention_core",
    )(qk, qk, vg, vg, dmask, qdec, kdec, cdec)


def _diff_attn_kernel(q_ref, k_ref, v_ref, lam_ref, g_ref, o_ref, m_ref, l_ref, acc_ref,
                      *, tq, lam_init, heads):
    i = pl.program_id(2)
    dv = q_ref.shape[1] // heads
    dh = dv // 2
    m_ref[...] = jnp.full_like(m_ref, NEG_BIG)
    l_ref[...] = jnp.zeros_like(l_ref)
    acc_ref[...] = jnp.zeros_like(acc_ref)
    q = q_ref[...]

    def kv_step(j, on_diagonal):
        rows = pl.ds(pl.multiple_of(j * tq, tq), tq)
        kj = k_ref[rows, :]
        vj = v_ref[rows, :]
        if on_diagonal:
            visible = (lax.broadcasted_iota(jnp.int32, (tq, tq), 1)
                       <= lax.broadcasted_iota(jnp.int32, (tq, tq), 0))

        def scores(h):
            return [_dot_nt(q[:, h * dv + c * dh:h * dv + (c + 1) * dh],
                            kj[:, h * dv + c * dh:h * dv + (c + 1) * dh]) for c in range(2)]

        nxt = scores(0)
        for h in range(heads):
            cur = nxt
            if h + 1 < heads:
                nxt = scores(h + 1)
            ps = []
            corrs = []
            for c in range(2):
                s = cur[c]
                if on_diagonal:
                    s = jnp.where(visible, s, NEG_BIG)
                m_prev = m_ref[2 * h + c]
                m_next = jnp.maximum(m_prev, jnp.max(s, axis=1, keepdims=True))
                corr = jnp.exp(m_prev - m_next)
                p = jnp.exp(s - jnp.concatenate([m_next] * (tq // LANES), axis=1))
                fold = p[:, :LANES]
                for g in range(1, tq // LANES):
                    fold = fold + p[:, g * LANES:(g + 1) * LANES]
                l_ref[2 * h + c] = l_ref[2 * h + c] * corr + fold
                m_ref[2 * h + c] = m_next
                ps.append(p)
                corrs.append(corr)
            pv = _dot(jnp.concatenate(ps, axis=0).astype(BF16), vj[:, h * dv:(h + 1) * dv])
            acc_ref[2 * h] = acc_ref[2 * h] * corrs[0] + pv[:tq]
            acc_ref[2 * h + 1] = acc_ref[2 * h + 1] * corrs[1] + pv[tq:]

    def off_diagonal(j, carry):
        kv_step(j, False)
        return carry

    lax.fori_loop(0, i, off_diagonal, 0)
    kv_step(i, True)

    lp = lam_ref[...]
    lam = (jnp.exp(jnp.sum(lp[0:1] * lp[1:2], axis=1, keepdims=True))
           - jnp.exp(jnp.sum(lp[2:3] * lp[3:4], axis=1, keepdims=True)) + lam_init)
    for h in range(heads):
        l1 = jnp.sum(l_ref[2 * h], axis=1, keepdims=True)
        l2 = jnp.sum(l_ref[2 * h + 1], axis=1, keepdims=True)
        o = acc_ref[2 * h] / l1 - lam * (acc_ref[2 * h + 1] / l2)
        o = o * lax.rsqrt(jnp.mean(o * o, axis=-1, keepdims=True) + LN_EPS)
        o_ref[:, h * dv:(h + 1) * dv] = (o * g_ref[...] * (1.0 - lam_init)).astype(o_ref.dtype)


def _diff_attention_core(q, k, v, lam_p, subln_g, *, batch, seq, lam_init, tq, heads):
    dv = v.shape[1] // DIFF_HEADS
    q_tiles = seq // tq
    width = heads * dv
    stat = pltpu.VMEM((2 * heads, tq, dv), F32)
    return pl.pallas_call(
        functools.partial(_diff_attn_kernel, tq=tq, lam_init=lam_init, heads=heads),
        grid=(batch, DIFF_HEADS // heads, q_tiles),
        in_specs=[pl.BlockSpec((tq, width), lambda b, hh, i: (b * q_tiles + i, hh)),
                  pl.BlockSpec((seq, width), lambda b, hh, i: (b, hh)),
                  pl.BlockSpec((seq, width), lambda b, hh, i: (b, hh)),
                  pl.BlockSpec(lam_p.shape, lambda b, hh, i: (0, 0)),
                  pl.BlockSpec((1, dv), lambda b, hh, i: (0, 0))],
        out_specs=pl.BlockSpec((tq, width), lambda b, hh, i: (b * q_tiles + i, hh)),
        out_shape=jax.ShapeDtypeStruct(q.shape, BF16),
        scratch_shapes=[stat] * 3,
        compiler_params=_params(("parallel", "parallel", "arbitrary")),
        name="diff_attention_core",
    )(q, k, v, lam_p, subln_g.reshape(1, dv))


def _descending_top(s, count):
    tops = []
    cur = s
    rank = jnp.full(s.shape, float(count), F32)
    for r in range(count):
        m = jnp.max(cur, axis=0, keepdims=True)
        tops.append(m)
        hit = cur >= m
        rank = jnp.where(hit, float(r), rank)
        cur = jnp.where(hit, NEG_BIG, cur)
    return tops, rank


def _peer_route_kernel(x_ref, wq_ref, keys_ref, xt_ref, r2_ref, e2_ref, n_ref, coef_ref):
    k = PEER_TOPK
    x = x_ref[...]
    xt_ref[0] = x.T
    qt = _dot_nt(wq_ref[...], x).astype(BF16)
    half = keys_ref.shape[2]
    for h in range(PEER_HEADS):
        s1 = _dot(keys_ref[2 * h], qt[(2 * h) * half:(2 * h + 1) * half, :])
        s2 = _dot(keys_ref[2 * h + 1], qt[(2 * h + 1) * half:(2 * h + 2) * half, :])
        a, _ = _descending_top(s1, k)
        b, rank2 = _descending_top(s2, k)
        a_all = jnp.concatenate(a, axis=0)
        b_all = jnp.concatenate(b, axis=0)
        cands = [a[0] + b_all]
        cands += [a[i] + b_all[:k // 2] for i in range(1, k // 2)]
        cands += [a_all[k // 2:] + b[0]]
        cand = jnp.concatenate(cands, axis=0)
        tau = _descending_top(cand, k)[0][k - 1]
        top = a[0] + b[0]
        z = jnp.sum(jnp.where(cand >= tau, jnp.exp(cand - top), 0.0), axis=0, keepdims=True)
        thr = tau - s1
        n = jnp.zeros_like(s1)
        for jj in range(k):
            n = jnp.where(b[jj] >= thr, float(jj + 1), n)
        r2_ref[0, h] = rank2
        e2_ref[0, h] = jnp.exp(s2 - b[0])
        n_ref[0, h] = n
        coef_ref[0, h] = jnp.exp(s1 - a[0]) / z


def _peer_route(x_bf, wq_t, keys, *, tm):
    t, d = x_bf.shape
    nk = keys.shape[1]
    heads = PEER_HEADS
    tok = pl.BlockSpec((1, heads, nk, tm), lambda i: (i, 0, 0, 0))
    tok_f32 = jax.ShapeDtypeStruct((t // tm, heads, nk, tm), F32)
    return pl.pallas_call(
        _peer_route_kernel,
        grid=(t // tm,),
        in_specs=[pl.BlockSpec((tm, d), lambda i: (i, 0)),
                  pl.BlockSpec(wq_t.shape, lambda i: (0, 0)),
                  pl.BlockSpec(keys.shape, lambda i: (0, 0, 0))],
        out_specs=[pl.BlockSpec((1, d, tm), lambda i: (i, 0, 0)), tok, tok, tok, tok],
        out_shape=[jax.ShapeDtypeStruct((t // tm, d, tm), BF16), tok_f32, tok_f32, tok_f32, tok_f32],
        compiler_params=_params(("parallel",)),
        name="peer_route",
    )(x_bf, wq_t, keys)


def _gelu_tanh(x):
    inner = x * (0.7978845608028654 + 0.035677408136300125 * (x * x))
    hx = 0.5 * x
    return hx + hx * jnp.tanh(inner)


BF16_ROWS = 16


def _peer_expert_kernel(xt_ref, u_ref, vt_ref, r2f_ref, e2f_ref, n_ref, coef_ref, x_ref, g_ref, b_ref,
                        o_ref, obf_ref, acc_ref, r2_ref, e2_ref, *sub_refs, alpha, sub):
    j = pl.program_id(1)
    nk = r2_ref.shape[1]
    groups = nk // BF16_ROWS
    tm = xt_ref.shape[2]
    te = u_ref.shape[0]
    n_sub = te // sub
    ht_refs = sub_refs[:n_sub]
    pt_refs = sub_refs[n_sub:]

    @pl.when(j == 0)
    def _():
        acc_ref[...] = jnp.zeros_like(acc_ref)
        r2_ref[...] = r2f_ref[0].astype(BF16)
        e2_ref[...] = e2f_ref[0].astype(BF16)

    def first_matmul(sb):
        ht_refs[sb][...] = _dot(u_ref[sb * sub:(sb + 1) * sub, :], xt_ref[0])

    first_matmul(0)
    for sb in range(n_sub):
        blk = slice(sb * sub, (sb + 1) * sub)
        ht_ref = ht_refs[sb]
        pt_ref = pt_refs[sb]
        if sb + 1 < n_sub:
            first_matmul(sb + 1)
        for al in range(sub // nk):
            a = sb * (sub // nk) + al
            rows = slice(al * nk, (al + 1) * nk)
            for lc in range(tm // LANES):
                lanes = slice(lc * LANES, (lc + 1) * LANES)
                w = jnp.zeros((nk, LANES), BF16)
                for h in range(PEER_HEADS):
                    nb = jnp.broadcast_to(n_ref[0, h, a:a + 1, lanes], (BF16_ROWS, LANES)).astype(BF16)
                    cb = jnp.broadcast_to(coef_ref[0, h, a:a + 1, lanes], (BF16_ROWS, LANES)).astype(BF16)
                    nb = jnp.concatenate([nb] * groups, axis=0)
                    cb = jnp.concatenate([cb] * groups, axis=0)
                    w = w + jnp.where(r2_ref[h, :, lanes] < nb, e2_ref[h, :, lanes], 0.0) * cb
                act = _gelu_tanh(ht_ref[rows, lanes]).astype(BF16)
                pt_ref[rows, lanes] = w * act
        acc_ref[...] += _dot(vt_ref[0, :, blk], pt_ref[...])

    @pl.when(j == pl.num_programs(1) - 1)
    def _():
        out = _layer_norm_rows(alpha * x_ref[...] + acc_ref[...].T, g_ref[...], b_ref[...])
        o_ref[...] = out
        obf_ref[...] = out.astype(BF16)


def _peer_experts(xt, u, v, r2, e2, n, coef, x, g, b, *, alpha, te, sub):
    n_tiles, d, tm = xt.shape
    t = n_tiles * tm
    n_exp = u.shape[0]
    _, heads, nk, _ = n.shape
    vt = v.reshape(n_exp // te, te, d).transpose(0, 2, 1)
    row = pl.BlockSpec((tm, d), lambda i, j: (i, 0))
    vec = pl.BlockSpec((1, d), lambda i, j: (0, 0))
    tok = pl.BlockSpec((1, heads, nk, tm), lambda i, j: (i, 0, 0, 0))
    blk = pl.BlockSpec((1, heads, te // nk, tm), lambda i, j: (i, 0, j, 0))
    return pl.pallas_call(
        functools.partial(_peer_expert_kernel, alpha=alpha, sub=sub),
        grid=(t // tm, n_exp // te),
        in_specs=[pl.BlockSpec((1, d, tm), lambda i, j: (i, 0, 0)),
                  pl.BlockSpec((te, d), lambda i, j: (j, 0)),
                  pl.BlockSpec((1, d, te), lambda i, j: (j, 0, 0)),
                  tok, tok, blk, blk, row, vec, vec],
        out_specs=[row, row],
        out_shape=[jax.ShapeDtypeStruct((t, d), F32), jax.ShapeDtypeStruct((t, d), BF16)],
        scratch_shapes=([pltpu.VMEM((d, tm), F32),
                         pltpu.VMEM((heads, nk, tm), BF16), pltpu.VMEM((heads, nk, tm), BF16)]
                        + [pltpu.VMEM((sub, tm), F32)] * (te // sub)
                        + [pltpu.VMEM((sub, tm), BF16)] * (te // sub)),
        compiler_params=_params(("parallel", "arbitrary")),
        name="peer_experts",
    )(xt, u, vt, r2, e2, n, coef, x, g.reshape(1, d), b.reshape(1, d))


def _tiles(batch, seq):
    t = batch * seq
    return dict(
        mm_tm=min(seq, 1024), mm_tn=512,
        ln_tm=min(t, 512),
        attn_tq=min(seq, 256), attn_heads=4,
        peer_tm=min(t, 512), exp_te=1024, exp_sub=256,
    )


def _rope_tables_full(seq, dim, theta):
    pos = jnp.arange(seq, dtype=F32)
    freqs = 1.0 / (theta ** jnp.linspace(0.0, 1.0, dim // 2, dtype=F32))
    ang = pos[:, None] * freqs[None, :]
    return jnp.cos(ang), jnp.sin(ang)


def _rope_tables_partial(seq, head_dim, rope_dim, theta):
    pos = jnp.arange(seq, dtype=F32)
    freqs = theta ** (-jnp.arange(0, rope_dim, 2, dtype=F32) / rope_dim)
    ang = pos[:, None] * freqs[None, :]
    r2 = rope_dim // 2
    lane = jnp.arange(LANES) % head_dim
    cos = jnp.take(jnp.cos(ang), lane % r2, axis=1)
    sin = jnp.take(jnp.sin(ang), lane % r2, axis=1)
    first = (lane < r2)[None, :]
    second = ((lane >= r2) & (lane < rope_dim))[None, :]
    c = jnp.where(first | second, cos, 1.0)
    s_next = jnp.where(first, -sin, 0.0)
    s_prev = jnp.where(second, sin, 0.0)
    return c, s_next, s_prev, r2


def kernel(x, ret_w_in, ret_w_out, kv_w, diff_w_q, diff_lambda, diff_subln_g, diff_w_out,
           peer_w_q, peer_subkeys, peer_u, peer_v, ln_g, ln_b):
    batch, seq, d = x.shape
    t = batch * seq
    depth = peer_w_q.shape[0]
    n_a = ret_w_in.shape[0]
    alpha = (2 * depth) ** 0.25
    tl = _tiles(batch, seq)

    ret_dk = d // RET_HEADS
    hq = RET_HEADS * ret_dk
    diff_dh = d // (2 * DIFF_HEADS)
    kw = DIFF_HEADS * 2 * diff_dh
    ret_cos, ret_sin = _rope_tables_full(seq, ret_dk, RET_THETA)
    dc, ds_next, ds_prev, rot = _rope_tables_partial(seq, diff_dh, diff_dh // 4, ROPE_THETA)

    xf = x.reshape(t, d)
    xb = xf.astype(BF16)
    k_sh = v_sh = None
    for l in range(depth):
        if l < n_a:
            w_in = ret_w_in[l].astype(BF16)
            qk = _matmul_rope_full(xb, w_in[:, :2 * hq], ret_cos, ret_sin, seq=seq, head_dim=ret_dk,
                                   n_unscaled_cols=hq, scale=ret_dk ** -0.5,
                                   tm=tl["mm_tm"], tn=tl["mm_tn"])
            vg = _matmul(xb, w_in[:, 2 * hq:], tm=tl["mm_tm"], tn=tl["mm_tn"])
            y = _retention_core(qk, vg, batch=batch, seq=seq)
            w_out = ret_w_out[l].astype(BF16)
        else:
            j = l - n_a
            q = _matmul_rope_partial(xb, diff_w_q[j].astype(BF16), (dc, ds_next, ds_prev), seq=seq, rot=rot,
                                     scale=diff_dh ** -0.5, tm=tl["mm_tm"], tn=tl["mm_tn"])
            lam_init = 0.8 - 0.6 * math.exp(-0.3 * l)
            y = _diff_attention_core(q, k_sh, v_sh, diff_lambda[j], diff_subln_g[j], batch=batch, seq=seq,
                                     lam_init=lam_init, tq=tl["attn_tq"], heads=tl["attn_heads"])
            w_out = diff_w_out[j].astype(BF16)
        xf, xb = _matmul_residual_ln(y, w_out, xf, ln_g[l, 0], ln_b[l, 0], alpha=alpha, tm=tl["ln_tm"])

        n_keys = peer_subkeys.shape[3]
        keys = peer_subkeys[l].reshape(2 * PEER_HEADS, n_keys, -1).astype(BF16)
        xt, r2, e2, n, coef = _peer_route(xb, peer_w_q[l].T.astype(BF16), keys, tm=tl["peer_tm"])
        xf, xb = _peer_experts(xt, peer_u[l].astype(BF16), peer_v[l].astype(BF16), r2, e2, n, coef,
                               xf, ln_g[l, 1], ln_b[l, 1], alpha=alpha,
                               te=tl["exp_te"], sub=tl["exp_sub"])
        if l == n_a - 1:
            kv_wb = kv_w.astype(BF16)
            k_sh = _matmul_rope_partial(xb, kv_wb[:, :kw], (dc, ds_next, ds_prev), seq=seq, rot=rot,
                                        scale=1.0, tm=tl["mm_tm"], tn=tl["mm_tn"])
            v_sh = _matmul(xb, kv_wb[:, kw:], tm=tl["mm_tm"], tn=tl["mm_tn"])
    return xf.reshape(batch, seq, d)
```

```python
import functools
import math

import jax
import jax.numpy as jnp
from jax import lax
from jax.experimental import pallas as pl
from jax.experimental.pallas import tpu as pltpu

RET_HEADS = 4
RET_CHUNK = 128
RET_THETA = 10000.0
DIFF_HEADS = 8
ROPE_THETA = 500000.0
PEER_HEADS = 8
PEER_N_KEYS = 128
PEER_TOPK = 16
LN_EPS = 1e-5

LANES = 128
VMEM_LIMIT = 56 * 1024 * 1024
NEG_BIG = -1e30

F32 = jnp.float32
BF16 = jnp.bfloat16


def _params(sem, vmem=VMEM_LIMIT, flags=None):
    return pltpu.CompilerParams(dimension_semantics=sem, vmem_limit_bytes=vmem, flags=flags)


def _dot(a, b):
    return jnp.dot(a, b, preferred_element_type=F32)


def _dot_nt(a, b):
    return lax.dot_general(a, b, (((1,), (1,)), ((), ())), preferred_element_type=F32)


def _dot_tn(a, b):
    return lax.dot_general(a, b, (((0,), (0,)), ((), ())), preferred_element_type=F32)


def _mm_kernel(x_ref, w_ref, o_ref):
    o_ref[...] = _dot(x_ref[...], w_ref[...]).astype(o_ref.dtype)


def _matmul(x, w, *, tm, tn, out_dtype=BF16):
    t, k = x.shape
    n = w.shape[1]
    return pl.pallas_call(
        _mm_kernel,
        grid=(t // tm, n // tn),
        in_specs=[pl.BlockSpec((tm, k), lambda i, j: (i, 0)),
                  pl.BlockSpec((k, tn), lambda i, j: (0, j))],
        out_specs=pl.BlockSpec((tm, tn), lambda i, j: (i, j)),
        out_shape=jax.ShapeDtypeStruct((t, n), out_dtype),
        compiler_params=_params(("parallel", "arbitrary")),
        name="matmul",
    )(x, w)


def _mm_rope_full_kernel(x_ref, w_ref, cos_ref, sin_ref, o_ref, *, head_dim, n_unscaled_tiles, scale):
    acc = _dot(x_ref[...], w_ref[...])
    cos = cos_ref[...]
    sin = sin_ref[...]
    half = head_dim // 2
    s = jnp.where(pl.program_id(1) >= n_unscaled_tiles, scale, 1.0).astype(F32)
    for h in range(acc.shape[1] // head_dim):
        x1 = acc[:, h * head_dim:h * head_dim + half]
        x2 = acc[:, h * head_dim + half:(h + 1) * head_dim]
        o_ref[:, h * head_dim:h * head_dim + half] = ((x1 * cos - x2 * sin) * s).astype(o_ref.dtype)
        o_ref[:, h * head_dim + half:(h + 1) * head_dim] = ((x1 * sin + x2 * cos) * s).astype(o_ref.dtype)


def _matmul_rope_full(x, w, cos, sin, *, seq, head_dim, n_unscaled_cols, scale, tm, tn):
    t, k = x.shape
    n = w.shape[1]
    pos_tiles = seq // tm
    kern = functools.partial(_mm_rope_full_kernel, head_dim=head_dim,
                             n_unscaled_tiles=n_unscaled_cols // tn, scale=scale)
    return pl.pallas_call(
        kern,
        grid=(t // tm, n // tn),
        in_specs=[pl.BlockSpec((tm, k), lambda i, j: (i, 0)),
                  pl.BlockSpec((k, tn), lambda i, j: (0, j)),
                  pl.BlockSpec((tm, head_dim // 2), lambda i, j: (i % pos_tiles, 0)),
                  pl.BlockSpec((tm, head_dim // 2), lambda i, j: (i % pos_tiles, 0))],
        out_specs=pl.BlockSpec((tm, tn), lambda i, j: (i, j)),
        out_shape=jax.ShapeDtypeStruct((t, n), BF16),
        compiler_params=_params(("parallel", "arbitrary")),
        name="matmul_rope_full",
    )(x, w, cos, sin)


def _mm_rope_partial_kernel(x_ref, w_ref, c_ref, s_next_ref, s_prev_ref, o_ref, *, rot, scale):
    acc = _dot(x_ref[...], w_ref[...])
    c = c_ref[...]
    s_next = s_next_ref[...]
    s_prev = s_prev_ref[...]
    for g in range(acc.shape[1] // LANES):
        xg = acc[:, g * LANES:(g + 1) * LANES]
        nxt = pltpu.roll(xg, LANES - rot, axis=1)
        prv = pltpu.roll(xg, rot, axis=1)
        o_ref[:, g * LANES:(g + 1) * LANES] = ((xg * c + nxt * s_next + prv * s_prev) * scale).astype(o_ref.dtype)


def _matmul_rope_partial(x, w, tables, *, seq, rot, scale, tm, tn):
    t, k = x.shape
    n = w.shape[1]
    pos_tiles = seq // tm
    kern = functools.partial(_mm_rope_partial_kernel, rot=rot, scale=scale)
    tab_spec = pl.BlockSpec((tm, LANES), lambda i, j: (i % pos_tiles, 0))
    return pl.pallas_call(
        kern,
        grid=(t // tm, n // tn),
        in_specs=[pl.BlockSpec((tm, k), lambda i, j: (i, 0)),
                  pl.BlockSpec((k, tn), lambda i, j: (0, j)),
                  tab_spec, tab_spec, tab_spec],
        out_specs=pl.BlockSpec((tm, tn), lambda i, j: (i, j)),
        out_shape=jax.ShapeDtypeStruct((t, n), BF16),
        compiler_params=_params(("parallel", "arbitrary")),
        name="matmul_rope_partial",
    )(x, w, *tables)


def _layer_norm_rows(z, g, b):
    mu = jnp.mean(z, axis=-1, keepdims=True)
    zc = z - mu
    var = jnp.mean(zc * zc, axis=-1, keepdims=True)
    return zc * lax.rsqrt(var + LN_EPS) * g + b


def _mm_res_ln_kernel(y_ref, w_ref, x_ref, g_ref, b_ref, o_ref, obf_ref, *, alpha):
    mix = _dot(y_ref[...], w_ref[...])
    out = _layer_norm_rows(alpha * x_ref[...] + mix, g_ref[...], b_ref[...])
    o_ref[...] = out
    obf_ref[...] = out.astype(BF16)


def _matmul_residual_ln(y, w, x, g, b, *, alpha, tm):
    t, k = y.shape
    d = w.shape[1]
    row = pl.BlockSpec((tm, d), lambda i: (i, 0))
    vec = pl.BlockSpec((1, d), lambda i: (0, 0))
    return pl.pallas_call(
        functools.partial(_mm_res_ln_kernel, alpha=alpha),
        grid=(t // tm,),
        in_specs=[pl.BlockSpec((tm, k), lambda i: (i, 0)),
                  pl.BlockSpec((k, d), lambda i: (0, 0)),
                  row, vec, vec],
        out_specs=[row, row],
        out_shape=[jax.ShapeDtypeStruct((t, d), F32), jax.ShapeDtypeStruct((t, d), BF16)],
        compiler_params=_params(("parallel",)),
        name="matmul_residual_ln",
    )(y, w, x, g.reshape(1, d), b.reshape(1, d))


def _retention_kernel(q_ref, k_ref, v_ref, g_ref, dmask_ref, qdec_ref, kdec_ref, cdec_ref,
                      o_ref, state_ref, *, n_chunks):
    c = RET_CHUNK
    state_ref[...] = jnp.zeros_like(state_ref)
    dmask = dmask_ref[0]
    qdec = qdec_ref[0]
    kdec = kdec_ref[0]
    cdec = cdec_ref[0]

    def chunk(ci, carry):
        rows = pl.ds(pl.multiple_of(ci * c, c), c)
        qi = q_ref[rows, :]
        ki = k_ref[rows, :]
        vi = v_ref[rows, :]
        state = state_ref[...]
        sc = _dot_nt(qi, ki) * dmask
        intra = _dot(sc.astype(BF16), vi)
        inter = _dot(qi, state.astype(BF16)) * qdec
        kd = (ki.astype(F32) * kdec).astype(BF16)
        state_ref[...] = state * cdec + _dot_tn(kd, vi)
        y = intra + inter
        mu = jnp.mean(y, axis=-1, keepdims=True)
        yc = y - mu
        var = jnp.mean(yc * yc, axis=-1, keepdims=True)
        yn = yc * lax.rsqrt(var + LN_EPS)
        gate = g_ref[rows, :].astype(F32)
        gate = gate * jax.nn.sigmoid(gate)
        o_ref[rows, :] = (gate * yn).astype(o_ref.dtype)
        return carry

    lax.fori_loop(0, n_chunks, chunk, 0)


def _retention_core(qk, vg, *, batch, seq):
    h = RET_HEADS
    dk = qk.shape[1] // (2 * h)
    dv = vg.shape[1] // (2 * h)
    c = RET_CHUNK
    log_g = jnp.log(1.0 - jnp.exp2(-5.0 - jnp.arange(h, dtype=F32)))
    ar = jnp.arange(c, dtype=F32)
    rel = ar[:, None] - ar[None, :]
    dmask = jnp.where(rel[None] >= 0, jnp.exp(jnp.maximum(rel, 0.0)[None] * log_g[:, None, None]), 0.0)
    qdec = jnp.broadcast_to(jnp.exp((ar + 1.0)[None] * log_g[:, None])[:, :, None], (h, c, dv))
    kdec = jnp.broadcast_to(jnp.exp((c - 1.0 - ar)[None] * log_g[:, None])[:, :, None], (h, c, dk))
    cdec = jnp.broadcast_to(jnp.exp(c * log_g)[:, None, None], (h, 1, dv))
    head_tab = lambda shape: pl.BlockSpec((1,) + shape, lambda b, hh: (hh, 0, 0))
    return pl.pallas_call(
        functools.partial(_retention_kernel, n_chunks=seq // c),
        grid=(batch, h),
        in_specs=[pl.BlockSpec((seq, dk), lambda b, hh: (b, hh)),
                  pl.BlockSpec((seq, dk), lambda b, hh: (b, h + hh)),
                  pl.BlockSpec((seq, dv), lambda b, hh: (b, hh)),
                  pl.BlockSpec((seq, dv), lambda b, hh: (b, h + hh)),
                  head_tab((c, c)), head_tab((c, dv)), head_tab((c, dk)), head_tab((1, dv))],
        out_specs=pl.BlockSpec((seq, dv), lambda b, hh: (b, hh)),
        out_shape=jax.ShapeDtypeStruct((batch * seq, h * dv), BF16),
        scratch_shapes=[pltpu.VMEM((dk, dv), F32)],
        compiler_params=_params(("parallel", "arbitrary")),
        name="retention_core",
    )(qk, qk, vg, vg, dmask, qdec, kdec, cdec)


def _diff_attn_kernel(q_ref, k_ref, v_ref, lam_ref, g_ref, o_ref, m_ref, l_ref, acc_ref,
                      *, tq, lam_init, heads):
    i = pl.program_id(2)
    dv = q_ref.shape[1] // heads
    dh = dv // 2
    m_ref[...] = jnp.full_like(m_ref, NEG_BIG)
    l_ref[...] = jnp.zeros_like(l_ref)
    acc_ref[...] = jnp.zeros_like(acc_ref)
    q = q_ref[...]

    def kv_step(j, on_diagonal):
        rows = pl.ds(pl.multiple_of(j * tq, tq), tq)
        kj = k_ref[rows, :]
        vj = v_ref[rows, :]
        if on_diagonal:
            visible = (lax.broadcasted_iota(jnp.int32, (tq, tq), 1)
                       <= lax.broadcasted_iota(jnp.int32, (tq, tq), 0))

        def scores(h):
            return [_dot_nt(q[:, h * dv + c * dh:h * dv + (c + 1) * dh],
                            kj[:, h * dv + c * dh:h * dv + (c + 1) * dh]) for c in range(2)]

        nxt = scores(0)
        for h in range(heads):
            cur = nxt
            if h + 1 < heads:
                nxt = scores(h + 1)
            ps = []
            corrs = []
            for c in range(2):
                s = cur[c]
                if on_diagonal:
                    s = jnp.where(visible, s, NEG_BIG)
                m_prev = m_ref[2 * h + c]
                m_next = jnp.maximum(m_prev, jnp.max(s, axis=1, keepdims=True))
                corr = jnp.exp(m_prev - m_next)
                p = jnp.exp(s - jnp.concatenate([m_next] * (tq // LANES), axis=1))
                fold = p[:, :LANES]
                for g in range(1, tq // LANES):
                    fold = fold + p[:, g * LANES:(g + 1) * LANES]
                l_ref[2 * h + c] = l_ref[2 * h + c] * corr + fold
                m_ref[2 * h + c] = m_next
                ps.append(p)
                corrs.append(corr)
            pv = _dot(jnp.concatenate(ps, axis=0).astype(BF16), vj[:, h * dv:(h + 1) * dv])
            acc_ref[2 * h] = acc_ref[2 * h] * corrs[0] + pv[:tq]
            acc_ref[2 * h + 1] = acc_ref[2 * h + 1] * corrs[1] + pv[tq:]

    def off_diagonal(j, carry):
        kv_step(j, False)
        return carry

    lax.fori_loop(0, i, off_diagonal, 0)
    kv_step(i, True)

    lp = lam_ref[...]
    lam = (jnp.exp(jnp.sum(lp[0:1] * lp[1:2], axis=1, keepdims=True))
           - jnp.exp(jnp.sum(lp[2:3] * lp[3:4], axis=1, keepdims=True)) + lam_init)
    for h in range(heads):
        l1 = jnp.sum(l_ref[2 * h], axis=1, keepdims=True)
        l2 = jnp.sum(l_ref[2 * h + 1], axis=1, keepdims=True)
        o = acc_ref[2 * h] / l1 - lam * (acc_ref[2 * h + 1] / l2)
        o = o * lax.rsqrt(jnp.mean(o * o, axis=-1, keepdims=True) + LN_EPS)
        o_ref[:, h * dv:(h + 1) * dv] = (o * g_ref[...] * (1.0 - lam_init)).astype(o_ref.dtype)


def _diff_attention_core(q, k, v, lam_p, subln_g, *, batch, seq, lam_init, tq, heads):
    dv = v.shape[1] // DIFF_HEADS
    q_tiles = seq // tq
    width = heads * dv
    stat = pltpu.VMEM((2 * heads, tq, dv), F32)
    return pl.pallas_call(
        functools.partial(_diff_attn_kernel, tq=tq, lam_init=lam_init, heads=heads),
        grid=(batch, DIFF_HEADS // heads, q_tiles),
        in_specs=[pl.BlockSpec((tq, width), lambda b, hh, i: (b * q_tiles + i, hh)),
                  pl.BlockSpec((seq, width), lambda b, hh, i: (b, hh)),
                  pl.BlockSpec((seq, width), lambda b, hh, i: (b, hh)),
                  pl.BlockSpec(lam_p.shape, lambda b, hh, i: (0, 0)),
                  pl.BlockSpec((1, dv), lambda b, hh, i: (0, 0))],
        out_specs=pl.BlockSpec((tq, width), lambda b, hh, i: (b * q_tiles + i, hh)),
        out_shape=jax.ShapeDtypeStruct(q.shape, BF16),
        scratch_shapes=[stat] * 3,
        compiler_params=_params(("parallel", "parallel", "arbitrary")),
        name="diff_attention_core",
    )(q, k, v, lam_p, subln_g.reshape(1, dv))


def _descending_top(s, count):
    tops = []
    cur = s
    for _ in range(count):
        m = jnp.max(cur, axis=0, keepdims=True)
        tops.append(m)
        cur = jnp.where(cur >= m, NEG_BIG, cur)
    return tops


def _peer_route_kernel(x_ref, wq_ref, keys_ref, xt_ref, e2_ref, thr_ref, coef_ref):
    k = PEER_TOPK
    x = x_ref[...]
    xt_ref[0] = x.T
    qt = _dot_nt(wq_ref[...], x).astype(BF16)
    half = keys_ref.shape[2]
    for h in range(PEER_HEADS):
        s1 = _dot(keys_ref[2 * h], qt[(2 * h) * half:(2 * h + 1) * half, :])
        s2 = _dot(keys_ref[2 * h + 1], qt[(2 * h + 1) * half:(2 * h + 2) * half, :])
        a = _descending_top(s1, k)
        b = _descending_top(s2, k)
        a_all = jnp.concatenate(a, axis=0)
        b_all = jnp.concatenate(b, axis=0)
        cands = [a[0] + b_all]
        cands += [a[i] + b_all[:k // 2] for i in range(1, k // 2)]
        cands += [a_all[k // 2:] + b[0]]
        cand = jnp.concatenate(cands, axis=0)
        tau = _descending_top(cand, k)[k - 1]
        top = a[0] + b[0]
        z = jnp.sum(jnp.where(cand >= tau, jnp.exp(cand - top), 0.0), axis=0, keepdims=True)
        e2_ref[0, h] = jnp.exp(s2 - b[0])
        thr_ref[0, h] = jnp.exp((tau - b[0]) - s1)
        coef_ref[0, h] = jnp.exp(s1 - a[0]) / z


def _peer_route(x_bf, wq_t, keys, *, tm):
    t, d = x_bf.shape
    nk = keys.shape[1]
    heads = PEER_HEADS
    tok = pl.BlockSpec((1, heads, nk, tm), lambda i: (i, 0, 0, 0))
    tok_f32 = jax.ShapeDtypeStruct((t // tm, heads, nk, tm), F32)
    return pl.pallas_call(
        _peer_route_kernel,
        grid=(t // tm,),
        in_specs=[pl.BlockSpec((tm, d), lambda i: (i, 0)),
                  pl.BlockSpec(wq_t.shape, lambda i: (0, 0)),
                  pl.BlockSpec(keys.shape, lambda i: (0, 0, 0))],
        out_specs=[pl.BlockSpec((1, d, tm), lambda i: (i, 0, 0)), tok, tok, tok],
        out_shape=[jax.ShapeDtypeStruct((t // tm, d, tm), BF16), tok_f32, tok_f32, tok_f32],
        compiler_params=_params(("parallel",)),
        name="peer_route",
    )(x_bf, wq_t, keys)


def _gelu_tanh(x):
    inner = x * (0.7978845608028654 + 0.035677408136300125 * (x * x))
    hx = 0.5 * x
    return hx + hx * jnp.tanh(inner)


HALF_ROWS = 64


def _peer_expert_kernel(xt_ref, u_ref, vt_ref, e2_ref, thr_ref, coef_ref, x_ref, g_ref, b_ref,
                        o_ref, obf_ref, acc_ref, pt_ref, *ht_refs, alpha, sub):
    j = pl.program_id(1)
    nk = e2_ref.shape[2]
    tm = xt_ref.shape[2]
    te = u_ref.shape[0]
    n_sub = te // sub
    a_per_sub = sub // nk

    @pl.when(j == 0)
    def _():
        acc_ref[...] = jnp.zeros_like(acc_ref)

    def first_matmul(sb):
        ht_refs[sb][...] = _dot(u_ref[sb * sub:(sb + 1) * sub, :], xt_ref[0])

    first_matmul(0)
    for sb in range(n_sub):
        ht_ref = ht_refs[sb]
        if sb + 1 < n_sub:
            first_matmul(sb + 1)
        for lc in range(tm // LANES):
            lanes = slice(lc * LANES, (lc + 1) * LANES)
            for hb in range(nk // HALF_ROWS):
                keys = slice(hb * HALF_ROWS, (hb + 1) * HALF_ROWS)
                ws = [jnp.zeros((HALF_ROWS, LANES), F32)] * a_per_sub
                for h in range(PEER_HEADS):
                    e2 = e2_ref[0, h, keys, lanes]
                    for al in range(a_per_sub):
                        a = sb * a_per_sub + al
                        thr = thr_ref[0, h, a:a + 1, lanes]
                        coef = coef_ref[0, h, a:a + 1, lanes]
                        ws[al] = ws[al] + jnp.where(e2 >= thr, e2, 0.0) * coef
                for al in range(a_per_sub):
                    rows = slice(al * nk + hb * HALF_ROWS, al * nk + (hb + 1) * HALF_ROWS)
                    gated = (ws[al] * _gelu_tanh(ht_ref[rows, lanes])).astype(BF16)
                    pt_ref[sb * sub + rows.start:sb * sub + rows.stop, lanes] = gated
    acc_ref[...] += _dot(vt_ref[0], pt_ref[...])

    @pl.when(j == pl.num_programs(1) - 1)
    def _():
        out = _layer_norm_rows(alpha * x_ref[...] + acc_ref[...].T, g_ref[...], b_ref[...])
        o_ref[...] = out
        obf_ref[...] = out.astype(BF16)


def _peer_experts(xt, u, v, e2, thr, coef, x, g, b, *, alpha, te, sub):
    n_tiles, d, tm = xt.shape
    t = n_tiles * tm
    n_exp = u.shape[0]
    _, heads, nk, _ = e2.shape
    vt = v.reshape(n_exp // te, te, d).transpose(0, 2, 1)
    row = pl.BlockSpec((tm, d), lambda i, j: (i, 0))
    vec = pl.BlockSpec((1, d), lambda i, j: (0, 0))
    tok = pl.BlockSpec((1, heads, nk, tm), lambda i, j: (i, 0, 0, 0))
    blk = pl.BlockSpec((1, heads, te // nk, tm), lambda i, j: (i, 0, j, 0))
    return pl.pallas_call(
        functools.partial(_peer_expert_kernel, alpha=alpha, sub=sub),
        grid=(t // tm, n_exp // te),
        in_specs=[pl.BlockSpec((1, d, tm), lambda i, j: (i, 0, 0)),
                  pl.BlockSpec((te, d), lambda i, j: (j, 0)),
                  pl.BlockSpec((1, d, te), lambda i, j: (j, 0, 0)),
                  tok, blk, blk, row, vec, vec],
        out_specs=[row, row],
        out_shape=[jax.ShapeDtypeStruct((t, d), F32), jax.ShapeDtypeStruct((t, d), BF16)],
        scratch_shapes=([pltpu.VMEM((d, tm), F32), pltpu.VMEM((te, tm), BF16)]
                        + [pltpu.VMEM((sub, tm), F32)] * (te // sub)),
        compiler_params=_params(("parallel", "arbitrary")),
        name="peer_experts",
    )(xt, u, vt, e2, thr, coef, x, g.reshape(1, d), b.reshape(1, d))


def _tiles(batch, seq):
    t = batch * seq
    return dict(
        mm_tm=min(seq, 1024), mm_tn=512,
        ln_tm=min(t, 512),
        attn_tq=min(seq, 256), attn_heads=4,
        peer_tm=min(t, 512), exp_te=1024, exp_sub=256,
    )


def _rope_tables_full(seq, dim, theta):
    pos = jnp.arange(seq, dtype=F32)
    freqs = 1.0 / (theta ** jnp.linspace(0.0, 1.0, dim // 2, dtype=F32))
    ang = pos[:, None] * freqs[None, :]
    return jnp.cos(ang), jnp.sin(ang)


def _rope_tables_partial(seq, head_dim, rope_dim, theta):
    pos = jnp.arange(seq, dtype=F32)
    freqs = theta ** (-jnp.arange(0, rope_dim, 2, dtype=F32) / rope_dim)
    ang = pos[:, None] * freqs[None, :]
    r2 = rope_dim // 2
    lane = jnp.arange(LANES) % head_dim
    cos = jnp.take(jnp.cos(ang), lane % r2, axis=1)
    sin = jnp.take(jnp.sin(ang), lane % r2, axis=1)
    first = (lane < r2)[None, :]
    second = ((lane >= r2) & (lane < rope_dim))[None, :]
    c = jnp.where(first | second, cos, 1.0)
    s_next = jnp.where(first, -sin, 0.0)
    s_prev = jnp.where(second, sin, 0.0)
    return c, s_next, s_prev, r2


def kernel(x, ret_w_in, ret_w_out, kv_w, diff_w_q, diff_lambda, diff_subln_g, diff_w_out,
           peer_w_q, peer_subkeys, peer_u, peer_v, ln_g, ln_b):
    batch, seq, d = x.shape
    t = batch * seq
    depth = peer_w_q.shape[0]
    n_a = ret_w_in.shape[0]
    alpha = (2 * depth) ** 0.25
    tl = _tiles(batch, seq)

    ret_dk = d // RET_HEADS
    hq = RET_HEADS * ret_dk
    diff_dh = d // (2 * DIFF_HEADS)
    kw = DIFF_HEADS * 2 * diff_dh
    ret_cos, ret_sin = _rope_tables_full(seq, ret_dk, RET_THETA)
    dc, ds_next, ds_prev, rot = _rope_tables_partial(seq, diff_dh, diff_dh // 4, ROPE_THETA)

    xf = x.reshape(t, d)
    xb = xf.astype(BF16)
    k_sh = v_sh = None
    for l in range(depth):
        if l < n_a:
            w_in = ret_w_in[l].astype(BF16)
            qk = _matmul_rope_full(xb, w_in[:, :2 * hq], ret_cos, ret_sin, seq=seq, head_dim=ret_dk,
                                   n_unscaled_cols=hq, scale=ret_dk ** -0.5,
                                   tm=tl["mm_tm"], tn=tl["mm_tn"])
            vg = _matmul(xb, w_in[:, 2 * hq:], tm=tl["mm_tm"], tn=tl["mm_tn"])
            y = _retention_core(qk, vg, batch=batch, seq=seq)
            w_out = ret_w_out[l].astype(BF16)
        else:
            j = l - n_a
            q = _matmul_rope_partial(xb, diff_w_q[j].astype(BF16), (dc, ds_next, ds_prev), seq=seq, rot=rot,
                                     scale=diff_dh ** -0.5, tm=tl["mm_tm"], tn=tl["mm_tn"])
            lam_init = 0.8 - 0.6 * math.exp(-0.3 * l)
            y = _diff_attention_core(q, k_sh, v_sh, diff_lambda[j], diff_subln_g[j], batch=batch, seq=seq,
                                     lam_init=lam_init, tq=tl["attn_tq"], heads=tl["attn_heads"])
            w_out = diff_w_out[j].astype(BF16)
        xf, xb = _matmul_residual_ln(y, w_out, xf, ln_g[l, 0], ln_b[l, 0], alpha=alpha, tm=tl["ln_tm"])

        n_keys = peer_subkeys.shape[3]
        keys = peer_subkeys[l].reshape(2 * PEER_HEADS, n_keys, -1).astype(BF16)
        xt, e2, thr, coef = _peer_route(xb, peer_w_q[l].T.astype(BF16), keys, tm=tl["peer_tm"])
        xf, xb = _peer_experts(xt, peer_u[l].astype(BF16), peer_v[l].astype(BF16), e2, thr, coef,
                               xf, ln_g[l, 1], ln_b[l, 1], alpha=alpha,
                               te=tl["exp_te"], sub=tl["exp_sub"])
        if l == n_a - 1:
            kv_wb = kv_w.astype(BF16)
            k_sh = _matmul_rope_partial(xb, kv_wb[:, :kw], (dc, ds_next, ds_prev), seq=seq, rot=rot,
                                        scale=1.0, tm=tl["mm_tm"], tn=tl["mm_tn"])
            v_sh = _matmul(xb, kv_wb[:, kw:], tm=tl["mm_tm"], tn=tl["mm_tn"])
    return xf.reshape(batch, seq, d)
```

```python
import functools
import math

import jax
import jax.numpy as jnp
from jax import lax
from jax.experimental import pallas as pl
from jax.experimental.pallas import tpu as pltpu

RET_HEADS = 4
RET_CHUNK = 128
RET_THETA = 10000.0
DIFF_HEADS = 8
ROPE_THETA = 500000.0
PEER_HEADS = 8
PEER_N_KEYS = 128
PEER_TOPK = 16
LN_EPS = 1e-5

LANES = 128
VMEM_LIMIT = 56 * 1024 * 1024
NEG_BIG = -1e30

F32 = jnp.float32
BF16 = jnp.bfloat16


def _params(sem, vmem=VMEM_LIMIT, flags=None):
    return pltpu.CompilerParams(dimension_semantics=sem, vmem_limit_bytes=vmem, flags=flags)


def _dot(a, b):
    return jnp.dot(a, b, preferred_element_type=F32)


def _dot_nt(a, b):
    return lax.dot_general(a, b, (((1,), (1,)), ((), ())), preferred_element_type=F32)


def _dot_tn(a, b):
    return lax.dot_general(a, b, (((0,), (0,)), ((), ())), preferred_element_type=F32)


def _mm_kernel(x_ref, w_ref, o_ref):
    o_ref[...] = _dot(x_ref[...], w_ref[...]).astype(o_ref.dtype)


def _matmul(x, w, *, tm, tn, out_dtype=BF16):
    t, k = x.shape
    n = w.shape[1]
    return pl.pallas_call(
        _mm_kernel,
        grid=(t // tm, n // tn),
        in_specs=[pl.BlockSpec((tm, k), lambda i, j: (i, 0)),
                  pl.BlockSpec((k, tn), lambda i, j: (0, j))],
        out_specs=pl.BlockSpec((tm, tn), lambda i, j: (i, j)),
        out_shape=jax.ShapeDtypeStruct((t, n), out_dtype),
        compiler_params=_params(("parallel", "arbitrary")),
        name="matmul",
    )(x, w)


def _mm_rope_full_kernel(x_ref, w_ref, cos_ref, sin_ref, o_ref, *, head_dim, n_unscaled_tiles, scale):
    acc = _dot(x_ref[...], w_ref[...])
    cos = cos_ref[...]
    sin = sin_ref[...]
    half = head_dim // 2
    s = jnp.where(pl.program_id(1) >= n_unscaled_tiles, scale, 1.0).astype(F32)
    for h in range(acc.shape[1] // head_dim):
        x1 = acc[:, h * head_dim:h * head_dim + half]
        x2 = acc[:, h * head_dim + half:(h + 1) * head_dim]
        o_ref[:, h * head_dim:h * head_dim + half] = ((x1 * cos - x2 * sin) * s).astype(o_ref.dtype)
        o_ref[:, h * head_dim + half:(h + 1) * head_dim] = ((x1 * sin + x2 * cos) * s).astype(o_ref.dtype)


def _matmul_rope_full(x, w, cos, sin, *, seq, head_dim, n_unscaled_cols, scale, tm, tn):
    t, k = x.shape
    n = w.shape[1]
    pos_tiles = seq // tm
    kern = functools.partial(_mm_rope_full_kernel, head_dim=head_dim,
                             n_unscaled_tiles=n_unscaled_cols // tn, scale=scale)
    return pl.pallas_call(
        kern,
        grid=(t // tm, n // tn),
        in_specs=[pl.BlockSpec((tm, k), lambda i, j: (i, 0)),
                  pl.BlockSpec((k, tn), lambda i, j: (0, j)),
                  pl.BlockSpec((tm, head_dim // 2), lambda i, j: (i % pos_tiles, 0)),
                  pl.BlockSpec((tm, head_dim // 2), lambda i, j: (i % pos_tiles, 0))],
        out_specs=pl.BlockSpec((tm, tn), lambda i, j: (i, j)),
        out_shape=jax.ShapeDtypeStruct((t, n), BF16),
        compiler_params=_params(("parallel", "arbitrary")),
        name="matmul_rope_full",
    )(x, w, cos, sin)


def _mm_rope_partial_kernel(x_ref, w_ref, c_ref, s_next_ref, s_prev_ref, o_ref, *, rot, scale):
    acc = _dot(x_ref[...], w_ref[...])
    c = c_ref[...]
    s_next = s_next_ref[...]
    s_prev = s_prev_ref[...]
    for g in range(acc.shape[1] // LANES):
        xg = acc[:, g * LANES:(g + 1) * LANES]
        nxt = pltpu.roll(xg, LANES - rot, axis=1)
        prv = pltpu.roll(xg, rot, axis=1)
        o_ref[:, g * LANES:(g + 1) * LANES] = ((xg * c + nxt * s_next + prv * s_prev) * scale).astype(o_ref.dtype)


def _matmul_rope_partial(x, w, tables, *, seq, rot, scale, tm, tn):
    t, k = x.shape
    n = w.shape[1]
    pos_tiles = seq // tm
    kern = functools.partial(_mm_rope_partial_kernel, rot=rot, scale=scale)
    tab_spec = pl.BlockSpec((tm, LANES), lambda i, j: (i % pos_tiles, 0))
    return pl.pallas_call(
        kern,
        grid=(t // tm, n // tn),
        in_specs=[pl.BlockSpec((tm, k), lambda i, j: (i, 0)),
                  pl.BlockSpec((k, tn), lambda i, j: (0, j)),
                  tab_spec, tab_spec, tab_spec],
        out_specs=pl.BlockSpec((tm, tn), lambda i, j: (i, j)),
        out_shape=jax.ShapeDtypeStruct((t, n), BF16),
        compiler_params=_params(("parallel", "arbitrary")),
        name="matmul_rope_partial",
    )(x, w, *tables)


def _layer_norm_rows(z, g, b):
    mu = jnp.mean(z, axis=-1, keepdims=True)
    zc = z - mu
    var = jnp.mean(zc * zc, axis=-1, keepdims=True)
    return zc * lax.rsqrt(var + LN_EPS) * g + b


def _mm_res_ln_kernel(y_ref, w_ref, x_ref, g_ref, b_ref, o_ref, obf_ref, *, alpha):
    mix = _dot(y_ref[...], w_ref[...])
    out = _layer_norm_rows(alpha * x_ref[...] + mix, g_ref[...], b_ref[...])
    o_ref[...] = out
    obf_ref[...] = out.astype(BF16)


def _matmul_residual_ln(y, w, x, g, b, *, alpha, tm):
    t, k = y.shape
    d = w.shape[1]
    row = pl.BlockSpec((tm, d), lambda i: (i, 0))
    vec = pl.BlockSpec((1, d), lambda i: (0, 0))
    return pl.pallas_call(
        functools.partial(_mm_res_ln_kernel, alpha=alpha),
        grid=(t // tm,),
        in_specs=[pl.BlockSpec((tm, k), lambda i: (i, 0)),
                  pl.BlockSpec((k, d), lambda i: (0, 0)),
                  row, vec, vec],
        out_specs=[row, row],
        out_shape=[jax.ShapeDtypeStruct((t, d), F32), jax.ShapeDtypeStruct((t, d), BF16)],
        compiler_params=_params(("parallel",)),
        name="matmul_residual_ln",
    )(y, w, x, g.reshape(1, d), b.reshape(1, d))


def _retention_kernel(q_ref, k_ref, v_ref, g_ref, dmask_ref, qdec_ref, kdec_ref, cdec_ref,
                      o_ref, state_ref, *, n_chunks):
    c = RET_CHUNK
    state_ref[...] = jnp.zeros_like(state_ref)
    dmask = dmask_ref[0]
    qdec = qdec_ref[0]
    kdec = kdec_ref[0]
    cdec = cdec_ref[0]

    def chunk(ci, carry):
        rows = pl.ds(pl.multiple_of(ci * c, c), c)
        qi = q_ref[rows, :]
        ki = k_ref[rows, :]
        vi = v_ref[rows, :]
        state = state_ref[...]
        sc = _dot_nt(qi, ki) * dmask
        intra = _dot(sc.astype(BF16), vi)
        inter = _dot(qi, state.astype(BF16)) * qdec
        kd = (ki.astype(F32) * kdec).astype(BF16)
        state_ref[...] = state * cdec + _dot_tn(kd, vi)
        y = intra + inter
        mu = jnp.mean(y, axis=-1, keepdims=True)
        yc = y - mu
        var = jnp.mean(yc * yc, axis=-1, keepdims=True)
        yn = yc * lax.rsqrt(var + LN_EPS)
        gate = g_ref[rows, :].astype(F32)
        gate = gate * jax.nn.sigmoid(gate)
        o_ref[rows, :] = (gate * yn).astype(o_ref.dtype)
        return carry

    lax.fori_loop(0, n_chunks, chunk, 0)


def _retention_core(qk, vg, *, batch, seq):
    h = RET_HEADS
    dk = qk.shape[1] // (2 * h)
    dv = vg.shape[1] // (2 * h)
    c = RET_CHUNK
    log_g = jnp.log(1.0 - jnp.exp2(-5.0 - jnp.arange(h, dtype=F32)))
    ar = jnp.arange(c, dtype=F32)
    rel = ar[:, None] - ar[None, :]
    dmask = jnp.where(rel[None] >= 0, jnp.exp(jnp.maximum(rel, 0.0)[None] * log_g[:, None, None]), 0.0)
    qdec = jnp.broadcast_to(jnp.exp((ar + 1.0)[None] * log_g[:, None])[:, :, None], (h, c, dv))
    kdec = jnp.broadcast_to(jnp.exp((c - 1.0 - ar)[None] * log_g[:, None])[:, :, None], (h, c, dk))
    cdec = jnp.broadcast_to(jnp.exp(c * log_g)[:, None, None], (h, 1, dv))
    head_tab = lambda shape: pl.BlockSpec((1,) + shape, lambda b, hh: (hh, 0, 0))
    return pl.pallas_call(
        functools.partial(_retention_kernel, n_chunks=seq // c),
        grid=(batch, h),
        in_specs=[pl.BlockSpec((seq, dk), lambda b, hh: (b, hh)),
                  pl.BlockSpec((seq, dk), lambda b, hh: (b, h + hh)),
                  pl.BlockSpec((seq, dv), lambda b, hh: (b, hh)),
                  pl.BlockSpec((seq, dv), lambda b, hh: (b, h + hh)),
                  head_tab((c, c)), head_tab((c, dv)), head_tab((c, dk)), head_tab((1, dv))],
        out_specs=pl.BlockSpec((seq, dv), lambda b, hh: (b, hh)),
        out_shape=jax.ShapeDtypeStruct((batch * seq, h * dv), BF16),
        scratch_shapes=[pltpu.VMEM((dk, dv), F32)],
        compiler_params=_params(("parallel", "arbitrary")),
        name="retention_core",
    )(qk, qk, vg, vg, dmask, qdec, kdec, cdec)


def _diff_attn_kernel(q_ref, k_ref, v_ref, lam_ref, g_ref, o_ref, m_ref, l_ref, acc_ref,
                      *, tq, lam_init, heads):
    i = pl.program_id(2)
    dv = q_ref.shape[1] // heads
    dh = dv // 2
    m_ref[...] = jnp.full_like(m_ref, NEG_BIG)
    l_ref[...] = jnp.zeros_like(l_ref)
    acc_ref[...] = jnp.zeros_like(acc_ref)
    q = q_ref[...]

    def kv_step(j, on_diagonal):
        rows = pl.ds(pl.multiple_of(j * tq, tq), tq)
        kj = k_ref[rows, :]
        vj = v_ref[rows, :]
        if on_diagonal:
            visible = (lax.broadcasted_iota(jnp.int32, (tq, tq), 1)
                       <= lax.broadcasted_iota(jnp.int32, (tq, tq), 0))

        def scores(h):
            return [_dot_nt(q[:, h * dv + c * dh:h * dv + (c + 1) * dh],
                            kj[:, h * dv + c * dh:h * dv + (c + 1) * dh]) for c in range(2)]

        nxt = scores(0)
        for h in range(heads):
            cur = nxt
            if h + 1 < heads:
                nxt = scores(h + 1)
            ps = []
            corrs = []
            for c in range(2):
                s = cur[c]
                if on_diagonal:
                    s = jnp.where(visible, s, NEG_BIG)
                m_prev = m_ref[2 * h + c]
                m_next = jnp.maximum(m_prev, jnp.max(s, axis=1, keepdims=True))
                corr = jnp.exp(m_prev - m_next)
                p = jnp.exp(s - jnp.concatenate([m_next] * (tq // LANES), axis=1))
                fold = p[:, :LANES]
                for g in range(1, tq // LANES):
                    fold = fold + p[:, g * LANES:(g + 1) * LANES]
                l_ref[2 * h + c] = l_ref[2 * h + c] * corr + fold
                m_ref[2 * h + c] = m_next
                ps.append(p)
                corrs.append(corr)
            pv = _dot(jnp.concatenate(ps, axis=0).astype(BF16), vj[:, h * dv:(h + 1) * dv])
            acc_ref[2 * h] = acc_ref[2 * h] * corrs[0] + pv[:tq]
            acc_ref[2 * h + 1] = acc_ref[2 * h + 1] * corrs[1] + pv[tq:]

    def off_diagonal(j, carry):
        kv_step(j, False)
        return carry

    lax.fori_loop(0, i, off_diagonal, 0)
    kv_step(i, True)

    lp = lam_ref[...]
    lam = (jnp.exp(jnp.sum(lp[0:1] * lp[1:2], axis=1, keepdims=True))
           - jnp.exp(jnp.sum(lp[2:3] * lp[3:4], axis=1, keepdims=True)) + lam_init)
    for h in range(heads):
        l1 = jnp.sum(l_ref[2 * h], axis=1, keepdims=True)
        l2 = jnp.sum(l_ref[2 * h + 1], axis=1, keepdims=True)
        o = acc_ref[2 * h] / l1 - lam * (acc_ref[2 * h + 1] / l2)
        o = o * lax.rsqrt(jnp.mean(o * o, axis=-1, keepdims=True) + LN_EPS)
        o_ref[:, h * dv:(h + 1) * dv] = (o * g_ref[...] * (1.0 - lam_init)).astype(o_ref.dtype)


def _diff_attention_core(q, k, v, lam_p, subln_g, *, batch, seq, lam_init, tq, heads):
    dv = v.shape[1] // DIFF_HEADS
    q_tiles = seq // tq
    width = heads * dv
    stat = pltpu.VMEM((2 * heads, tq, dv), F32)
    return pl.pallas_call(
        functools.partial(_diff_attn_kernel, tq=tq, lam_init=lam_init, heads=heads),
        grid=(batch, DIFF_HEADS // heads, q_tiles),
        in_specs=[pl.BlockSpec((tq, width), lambda b, hh, i: (b * q_tiles + i, hh)),
                  pl.BlockSpec((seq, width), lambda b, hh, i: (b, hh)),
                  pl.BlockSpec((seq, width), lambda b, hh, i: (b, hh)),
                  pl.BlockSpec(lam_p.shape, lambda b, hh, i: (0, 0)),
                  pl.BlockSpec((1, dv), lambda b, hh, i: (0, 0))],
        out_specs=pl.BlockSpec((tq, width), lambda b, hh, i: (b * q_tiles + i, hh)),
        out_shape=jax.ShapeDtypeStruct(q.shape, BF16),
        scratch_shapes=[stat] * 3,
        compiler_params=_params(("parallel", "parallel", "arbitrary")),
        name="diff_attention_core",
    )(q, k, v, lam_p, subln_g.reshape(1, dv))


def _descending_top(s, count):
    tops = []
    cur = s
    for _ in range(count):
        m = jnp.max(cur, axis=0, keepdims=True)
        tops.append(m)
        cur = jnp.where(cur >= m, NEG_BIG, cur)
    return tops


def _peer_route_kernel(x_ref, wq_ref, keys_ref, xt_ref, e2_ref, thr_ref, coef_ref):
    k = PEER_TOPK
    x = x_ref[...]
    xt_ref[0] = x.T
    qt = _dot_nt(wq_ref[...], x).astype(BF16)
    half = keys_ref.shape[2]
    for h in range(PEER_HEADS):
        s1 = _dot(keys_ref[2 * h], qt[(2 * h) * half:(2 * h + 1) * half, :])
        s2 = _dot(keys_ref[2 * h + 1], qt[(2 * h + 1) * half:(2 * h + 2) * half, :])
        a = _descending_top(s1, k)
        b = _descending_top(s2, k)
        a_all = jnp.concatenate(a, axis=0)
        b_all = jnp.concatenate(b, axis=0)
        cands = [a[0] + b_all]
        cands += [a[i] + b_all[:k // 2] for i in range(1, k // 2)]
        cands += [a_all[k // 2:] + b[0]]
        cand = jnp.concatenate(cands, axis=0)
        tau = _descending_top(cand, k)[k - 1]
        top = a[0] + b[0]
        z = jnp.sum(jnp.where(cand >= tau, jnp.exp(cand - top), 0.0), axis=0, keepdims=True)
        e2_ref[0, h] = jnp.exp(s2 - b[0])
        thr_ref[0, h] = jnp.exp((tau - b[0]) - s1)
        coef_ref[0, h] = jnp.exp(s1 - a[0]) / z


def _peer_route(x_bf, wq_t, keys, *, tm):
    t, d = x_bf.shape
    nk = keys.shape[1]
    heads = PEER_HEADS
    tok = pl.BlockSpec((1, heads, nk, tm), lambda i: (i, 0, 0, 0))
    tok_f32 = jax.ShapeDtypeStruct((t // tm, heads, nk, tm), F32)
    return pl.pallas_call(
        _peer_route_kernel,
        grid=(t // tm,),
        in_specs=[pl.BlockSpec((tm, d), lambda i: (i, 0)),
                  pl.BlockSpec(wq_t.shape, lambda i: (0, 0)),
                  pl.BlockSpec(keys.shape, lambda i: (0, 0, 0))],
        out_specs=[pl.BlockSpec((1, d, tm), lambda i: (i, 0, 0)), tok, tok, tok],
        out_shape=[jax.ShapeDtypeStruct((t // tm, d, tm), BF16), tok_f32, tok_f32, tok_f32],
        compiler_params=_params(("parallel",)),
        name="peer_route",
    )(x_bf, wq_t, keys)


def _gelu_tanh(x):
    inner = x * (0.7978845608028654 + 0.035677408136300125 * (x * x))
    hx = 0.5 * x
    return hx + hx * jnp.tanh(inner)


HALF_ROWS = 64


def _peer_expert_kernel(xt_ref, u_ref, vt_ref, e2_ref, thr_ref, coef_ref, x_ref, g_ref, b_ref,
                        o_ref, obf_ref, acc_ref, pt_ref, *ht_refs, alpha, sub, group):
    j = pl.program_id(1)
    nk = e2_ref.shape[2]
    tm = xt_ref.shape[2]
    te = u_ref.shape[0]
    n_sub = te // sub
    a_per_sub = sub // nk
    ahead = 2

    @pl.when(j == 0)
    def _():
        acc_ref[...] = jnp.zeros_like(acc_ref)

    def first_matmul(sb):
        ht_refs[sb][...] = _dot(u_ref[sb * sub:(sb + 1) * sub, :], xt_ref[0])

    for sb in range(min(ahead, n_sub)):
        first_matmul(sb)
    for sb in range(n_sub):
        ht_ref = ht_refs[sb]
        if sb + ahead < n_sub:
            first_matmul(sb + ahead)
        for lc in range(tm // LANES):
            lanes = slice(lc * LANES, (lc + 1) * LANES)
            for hb in range(nk // HALF_ROWS):
                keys = slice(hb * HALF_ROWS, (hb + 1) * HALF_ROWS)
                ws = [jnp.zeros((HALF_ROWS, LANES), F32)] * a_per_sub
                for h in range(PEER_HEADS):
                    e2 = e2_ref[0, h, keys, lanes]
                    for al in range(a_per_sub):
                        a = sb * a_per_sub + al
                        thr = thr_ref[0, h, a:a + 1, lanes]
                        coef = coef_ref[0, h, a:a + 1, lanes]
                        ws[al] = ws[al] + jnp.where(e2 >= thr, e2, 0.0) * coef
                for al in range(a_per_sub):
                    rows = slice(al * nk + hb * HALF_ROWS, al * nk + (hb + 1) * HALF_ROWS)
                    gated = (ws[al] * _gelu_tanh(ht_ref[rows, lanes])).astype(BF16)
                    pt_ref[sb * sub + rows.start:sb * sub + rows.stop, lanes] = gated
        if (sb + 1) % group == 0:
            grp = slice((sb + 1 - group) * sub, (sb + 1) * sub)
            acc_ref[...] += _dot(vt_ref[0, :, grp], pt_ref[grp, :])

    @pl.when(j == pl.num_programs(1) - 1)
    def _():
        out = _layer_norm_rows(alpha * x_ref[...] + acc_ref[...].T, g_ref[...], b_ref[...])
        o_ref[...] = out
        obf_ref[...] = out.astype(BF16)


def _peer_experts(xt, u, v, e2, thr, coef, x, g, b, *, alpha, te, sub, group):
    n_tiles, d, tm = xt.shape
    t = n_tiles * tm
    n_exp = u.shape[0]
    _, heads, nk, _ = e2.shape
    vt = v.reshape(n_exp // te, te, d).transpose(0, 2, 1)
    row = pl.BlockSpec((tm, d), lambda i, j: (i, 0))
    vec = pl.BlockSpec((1, d), lambda i, j: (0, 0))
    tok = pl.BlockSpec((1, heads, nk, tm), lambda i, j: (i, 0, 0, 0))
    blk = pl.BlockSpec((1, heads, te // nk, tm), lambda i, j: (i, 0, j, 0))
    return pl.pallas_call(
        functools.partial(_peer_expert_kernel, alpha=alpha, sub=sub, group=group),
        grid=(t // tm, n_exp // te),
        in_specs=[pl.BlockSpec((1, d, tm), lambda i, j: (i, 0, 0)),
                  pl.BlockSpec((te, d), lambda i, j: (j, 0)),
                  pl.BlockSpec((1, d, te), lambda i, j: (j, 0, 0)),
                  tok, blk, blk, row, vec, vec],
        out_specs=[row, row],
        out_shape=[jax.ShapeDtypeStruct((t, d), F32), jax.ShapeDtypeStruct((t, d), BF16)],
        scratch_shapes=([pltpu.VMEM((d, tm), F32), pltpu.VMEM((te, tm), BF16)]
                        + [pltpu.VMEM((sub, tm), F32)] * (te // sub)),
        compiler_params=_params(("parallel", "arbitrary")),
        name="peer_experts",
    )(xt, u, vt, e2, thr, coef, x, g.reshape(1, d), b.reshape(1, d))


def _tiles(batch, seq):
    t = batch * seq
    return dict(
        mm_tm=min(seq, 1024), mm_tn=512,
        ln_tm=min(t, 512),
        attn_tq=min(seq, 256), attn_heads=4,
        peer_tm=min(t, 512), exp_te=2048, exp_sub=512, exp_group=2,
    )


def _rope_tables_full(seq, dim, theta):
    pos = jnp.arange(seq, dtype=F32)
    freqs = 1.0 / (theta ** jnp.linspace(0.0, 1.0, dim // 2, dtype=F32))
    ang = pos[:, None] * freqs[None, :]
    return jnp.cos(ang), jnp.sin(ang)


def _rope_tables_partial(seq, head_dim, rope_dim, theta):
    pos = jnp.arange(seq, dtype=F32)
    freqs = theta ** (-jnp.arange(0, rope_dim, 2, dtype=F32) / rope_dim)
    ang = pos[:, None] * freqs[None, :]
    r2 = rope_dim // 2
    lane = jnp.arange(LANES) % head_dim
    cos = jnp.take(jnp.cos(ang), lane % r2, axis=1)
    sin = jnp.take(jnp.sin(ang), lane % r2, axis=1)
    first = (lane < r2)[None, :]
    second = ((lane >= r2) & (lane < rope_dim))[None, :]
    c = jnp.where(first | second, cos, 1.0)
    s_next = jnp.where(first, -sin, 0.0)
    s_prev = jnp.where(second, sin, 0.0)
    return c, s_next, s_prev, r2


def kernel(x, ret_w_in, ret_w_out, kv_w, diff_w_q, diff_lambda, diff_subln_g, diff_w_out,
           peer_w_q, peer_subkeys, peer_u, peer_v, ln_g, ln_b):
    batch, seq, d = x.shape
    t = batch * seq
    depth = peer_w_q.shape[0]
    n_a = ret_w_in.shape[0]
    alpha = (2 * depth) ** 0.25
    tl = _tiles(batch, seq)

    ret_dk = d // RET_HEADS
    hq = RET_HEADS * ret_dk
    diff_dh = d // (2 * DIFF_HEADS)
    kw = DIFF_HEADS * 2 * diff_dh
    ret_cos, ret_sin = _rope_tables_full(seq, ret_dk, RET_THETA)
    dc, ds_next, ds_prev, rot = _rope_tables_partial(seq, diff_dh, diff_dh // 4, ROPE_THETA)

    xf = x.reshape(t, d)
    xb = xf.astype(BF16)
    k_sh = v_sh = None
    for l in range(depth):
        if l < n_a:
            w_in = ret_w_in[l].astype(BF16)
            qk = _matmul_rope_full(xb, w_in[:, :2 * hq], ret_cos, ret_sin, seq=seq, head_dim=ret_dk,
                                   n_unscaled_cols=hq, scale=ret_dk ** -0.5,
                                   tm=tl["mm_tm"], tn=tl["mm_tn"])
            vg = _matmul(xb, w_in[:, 2 * hq:], tm=tl["mm_tm"], tn=tl["mm_tn"])
            y = _retention_core(qk, vg, batch=batch, seq=seq)
            w_out = ret_w_out[l].astype(BF16)
        else:
            j = l - n_a
            q = _matmul_rope_partial(xb, diff_w_q[j].astype(BF16), (dc, ds_next, ds_prev), seq=seq, rot=rot,
                                     scale=diff_dh ** -0.5, tm=tl["mm_tm"], tn=tl["mm_tn"])
            lam_init = 0.8 - 0.6 * math.exp(-0.3 * l)
            y = _diff_attention_core(q, k_sh, v_sh, diff_lambda[j], diff_subln_g[j], batch=batch, seq=seq,
                                     lam_init=lam_init, tq=tl["attn_tq"], heads=tl["attn_heads"])
            w_out = diff_w_out[j].astype(BF16)
        xf, xb = _matmul_residual_ln(y, w_out, xf, ln_g[l, 0], ln_b[l, 0], alpha=alpha, tm=tl["ln_tm"])

        n_keys = peer_subkeys.shape[3]
        keys = peer_subkeys[l].reshape(2 * PEER_HEADS, n_keys, -1).astype(BF16)
        xt, e2, thr, coef = _peer_route(xb, peer_w_q[l].T.astype(BF16), keys, tm=tl["peer_tm"])
        xf, xb = _peer_experts(xt, peer_u[l].astype(BF16), peer_v[l].astype(BF16), e2, thr, coef,
                               xf, ln_g[l, 1], ln_b[l, 1], alpha=alpha,
                               te=tl["exp_te"], sub=tl["exp_sub"], group=tl["exp_group"])
        if l == n_a - 1:
            kv_wb = kv_w.astype(BF16)
            k_sh = _matmul_rope_partial(xb, kv_wb[:, :kw], (dc, ds_next, ds_prev), seq=seq, rot=rot,
                                        scale=1.0, tm=tl["mm_tm"], tn=tl["mm_tn"])
            v_sh = _matmul(xb, kv_wb[:, kw:], tm=tl["mm_tm"], tn=tl["mm_tn"])
    return xf.reshape(batch, seq, d)
```

```python
import functools
import math

import jax
import jax.numpy as jnp
from jax import lax
from jax.experimental import pallas as pl
from jax.experimental.pallas import tpu as pltpu

RET_HEADS = 4
RET_CHUNK = 128
RET_THETA = 10000.0
DIFF_HEADS = 8
ROPE_THETA = 500000.0
PEER_HEADS = 8
PEER_N_KEYS = 128
PEER_TOPK = 16
LN_EPS = 1e-5

LANES = 128
VMEM_LIMIT = 56 * 1024 * 1024
NEG_BIG = -1e30

F32 = jnp.float32
BF16 = jnp.bfloat16


def _params(sem, vmem=VMEM_LIMIT, flags=None):
    return pltpu.CompilerParams(dimension_semantics=sem, vmem_limit_bytes=vmem, flags=flags)


def _dot(a, b):
    return jnp.dot(a, b, preferred_element_type=F32)


def _dot_nt(a, b):
    return lax.dot_general(a, b, (((1,), (1,)), ((), ())), preferred_element_type=F32)


def _dot_tn(a, b):
    return lax.dot_general(a, b, (((0,), (0,)), ((), ())), preferred_element_type=F32)


def _mm_kernel(x_ref, w_ref, o_ref):
    o_ref[...] = _dot(x_ref[...], w_ref[...]).astype(o_ref.dtype)


def _matmul(x, w, *, tm, tn, out_dtype=BF16):
    t, k = x.shape
    n = w.shape[1]
    return pl.pallas_call(
        _mm_kernel,
        grid=(t // tm, n // tn),
        in_specs=[pl.BlockSpec((tm, k), lambda i, j: (i, 0)),
                  pl.BlockSpec((k, tn), lambda i, j: (0, j))],
        out_specs=pl.BlockSpec((tm, tn), lambda i, j: (i, j)),
        out_shape=jax.ShapeDtypeStruct((t, n), out_dtype),
        compiler_params=_params(("parallel", "arbitrary")),
        name="matmul",
    )(x, w)


def _mm_rope_full_kernel(x_ref, w_ref, cos_ref, sin_ref, o_ref, *, head_dim, n_unscaled_tiles, scale):
    acc = _dot(x_ref[...], w_ref[...])
    cos = cos_ref[...]
    sin = sin_ref[...]
    half = head_dim // 2
    s = jnp.where(pl.program_id(1) >= n_unscaled_tiles, scale, 1.0).astype(F32)
    for h in range(acc.shape[1] // head_dim):
        x1 = acc[:, h * head_dim:h * head_dim + half]
        x2 = acc[:, h * head_dim + half:(h + 1) * head_dim]
        o_ref[:, h * head_dim:h * head_dim + half] = ((x1 * cos - x2 * sin) * s).astype(o_ref.dtype)
        o_ref[:, h * head_dim + half:(h + 1) * head_dim] = ((x1 * sin + x2 * cos) * s).astype(o_ref.dtype)


def _matmul_rope_full(x, w, cos, sin, *, seq, head_dim, n_unscaled_cols, scale, tm, tn):
    t, k = x.shape
    n = w.shape[1]
    pos_tiles = seq // tm
    kern = functools.partial(_mm_rope_full_kernel, head_dim=head_dim,
                             n_unscaled_tiles=n_unscaled_cols // tn, scale=scale)
    return pl.pallas_call(
        kern,
        grid=(t // tm, n // tn),
        in_specs=[pl.BlockSpec((tm, k), lambda i, j: (i, 0)),
                  pl.BlockSpec((k, tn), lambda i, j: (0, j)),
                  pl.BlockSpec((tm, head_dim // 2), lambda i, j: (i % pos_tiles, 0)),
                  pl.BlockSpec((tm, head_dim // 2), lambda i, j: (i % pos_tiles, 0))],
        out_specs=pl.BlockSpec((tm, tn), lambda i, j: (i, j)),
        out_shape=jax.ShapeDtypeStruct((t, n), BF16),
        compiler_params=_params(("parallel", "arbitrary")),
        name="matmul_rope_full",
    )(x, w, cos, sin)


def _mm_rope_partial_kernel(x_ref, w_ref, c_ref, s_next_ref, s_prev_ref, o_ref, *, rot, scale):
    acc = _dot(x_ref[...], w_ref[...])
    c = c_ref[...]
    s_next = s_next_ref[...]
    s_prev = s_prev_ref[...]
    for g in range(acc.shape[1] // LANES):
        xg = acc[:, g * LANES:(g + 1) * LANES]
        nxt = pltpu.roll(xg, LANES - rot, axis=1)
        prv = pltpu.roll(xg, rot, axis=1)
        o_ref[:, g * LANES:(g + 1) * LANES] = ((xg * c + nxt * s_next + prv * s_prev) * scale).astype(o_ref.dtype)


def _matmul_rope_partial(x, w, tables, *, seq, rot, scale, tm, tn):
    t, k = x.shape
    n = w.shape[1]
    pos_tiles = seq // tm
    kern = functools.partial(_mm_rope_partial_kernel, rot=rot, scale=scale)
    tab_spec = pl.BlockSpec((tm, LANES), lambda i, j: (i % pos_tiles, 0))
    return pl.pallas_call(
        kern,
        grid=(t // tm, n // tn),
        in_specs=[pl.BlockSpec((tm, k), lambda i, j: (i, 0)),
                  pl.BlockSpec((k, tn), lambda i, j: (0, j)),
                  tab_spec, tab_spec, tab_spec],
        out_specs=pl.BlockSpec((tm, tn), lambda i, j: (i, j)),
        out_shape=jax.ShapeDtypeStruct((t, n), BF16),
        compiler_params=_params(("parallel", "arbitrary")),
        name="matmul_rope_partial",
    )(x, w, *tables)


def _layer_norm_rows(z, g, b):
    mu = jnp.mean(z, axis=-1, keepdims=True)
    zc = z - mu
    var = jnp.mean(zc * zc, axis=-1, keepdims=True)
    return zc * lax.rsqrt(var + LN_EPS) * g + b


def _mm_res_ln_kernel(y_ref, w_ref, x_ref, g_ref, b_ref, o_ref, obf_ref, *, alpha):
    mix = _dot(y_ref[...], w_ref[...])
    out = _layer_norm_rows(alpha * x_ref[...] + mix, g_ref[...], b_ref[...])
    o_ref[...] = out
    obf_ref[...] = out.astype(BF16)


def _matmul_residual_ln(y, w, x, g, b, *, alpha, tm):
    t, k = y.shape
    d = w.shape[1]
    row = pl.BlockSpec((tm, d), lambda i: (i, 0))
    vec = pl.BlockSpec((1, d), lambda i: (0, 0))
    return pl.pallas_call(
        functools.partial(_mm_res_ln_kernel, alpha=alpha),
        grid=(t // tm,),
        in_specs=[pl.BlockSpec((tm, k), lambda i: (i, 0)),
                  pl.BlockSpec((k, d), lambda i: (0, 0)),
                  row, vec, vec],
        out_specs=[row, row],
        out_shape=[jax.ShapeDtypeStruct((t, d), F32), jax.ShapeDtypeStruct((t, d), BF16)],
        compiler_params=_params(("parallel",)),
        name="matmul_residual_ln",
    )(y, w, x, g.reshape(1, d), b.reshape(1, d))


def _retention_kernel(q_ref, k_ref, v_ref, g_ref, dmask_ref, qdec_ref, kdec_ref, cdec_ref,
                      o_ref, state_ref, *, n_chunks):
    c = RET_CHUNK
    state_ref[...] = jnp.zeros_like(state_ref)
    dmask = dmask_ref[0]
    qdec = qdec_ref[0]
    kdec = kdec_ref[0]
    cdec = cdec_ref[0]

    def chunk(ci, carry):
        rows = pl.ds(pl.multiple_of(ci * c, c), c)
        qi = q_ref[rows, :]
        ki = k_ref[rows, :]
        vi = v_ref[rows, :]
        state = state_ref[...]
        sc = _dot_nt(qi, ki) * dmask
        intra = _dot(sc.astype(BF16), vi)
        inter = _dot(qi, state.astype(BF16)) * qdec
        kd = (ki.astype(F32) * kdec).astype(BF16)
        state_ref[...] = state * cdec + _dot_tn(kd, vi)
        y = intra + inter
        mu = jnp.mean(y, axis=-1, keepdims=True)
        yc = y - mu
        var = jnp.mean(yc * yc, axis=-1, keepdims=True)
        yn = yc * lax.rsqrt(var + LN_EPS)
        gate = g_ref[rows, :].astype(F32)
        gate = gate * jax.nn.sigmoid(gate)
        o_ref[rows, :] = (gate * yn).astype(o_ref.dtype)
        return carry

    lax.fori_loop(0, n_chunks, chunk, 0)


def _retention_core(qk, vg, *, batch, seq):
    h = RET_HEADS
    dk = qk.shape[1] // (2 * h)
    dv = vg.shape[1] // (2 * h)
    c = RET_CHUNK
    log_g = jnp.log(1.0 - jnp.exp2(-5.0 - jnp.arange(h, dtype=F32)))
    ar = jnp.arange(c, dtype=F32)
    rel = ar[:, None] - ar[None, :]
    dmask = jnp.where(rel[None] >= 0, jnp.exp(jnp.maximum(rel, 0.0)[None] * log_g[:, None, None]), 0.0)
    qdec = jnp.broadcast_to(jnp.exp((ar + 1.0)[None] * log_g[:, None])[:, :, None], (h, c, dv))
    kdec = jnp.broadcast_to(jnp.exp((c - 1.0 - ar)[None] * log_g[:, None])[:, :, None], (h, c, dk))
    cdec = jnp.broadcast_to(jnp.exp(c * log_g)[:, None, None], (h, 1, dv))
    head_tab = lambda shape: pl.BlockSpec((1,) + shape, lambda b, hh: (hh, 0, 0))
    return pl.pallas_call(
        functools.partial(_retention_kernel, n_chunks=seq // c),
        grid=(batch, h),
        in_specs=[pl.BlockSpec((seq, dk), lambda b, hh: (b, hh)),
                  pl.BlockSpec((seq, dk), lambda b, hh: (b, h + hh)),
                  pl.BlockSpec((seq, dv), lambda b, hh: (b, hh)),
                  pl.BlockSpec((seq, dv), lambda b, hh: (b, h + hh)),
                  head_tab((c, c)), head_tab((c, dv)), head_tab((c, dk)), head_tab((1, dv))],
        out_specs=pl.BlockSpec((seq, dv), lambda b, hh: (b, hh)),
        out_shape=jax.ShapeDtypeStruct((batch * seq, h * dv), BF16),
        scratch_shapes=[pltpu.VMEM((dk, dv), F32)],
        compiler_params=_params(("parallel", "arbitrary")),
        name="retention_core",
    )(qk, qk, vg, vg, dmask, qdec, kdec, cdec)


def _diff_attn_kernel(q_ref, k_ref, v_ref, lam_ref, g_ref, o_ref, m_ref, l_ref, acc_ref,
                      *, tq, lam_init, heads):
    i = pl.program_id(2)
    dv = q_ref.shape[1] // heads
    dh = dv // 2
    m_ref[...] = jnp.full_like(m_ref, NEG_BIG)
    l_ref[...] = jnp.zeros_like(l_ref)
    acc_ref[...] = jnp.zeros_like(acc_ref)
    q = q_ref[...]

    def kv_step(j, on_diagonal):
        rows = pl.ds(pl.multiple_of(j * tq, tq), tq)
        kj = k_ref[rows, :]
        vj = v_ref[rows, :]
        if on_diagonal:
            visible = (lax.broadcasted_iota(jnp.int32, (tq, tq), 1)
                       <= lax.broadcasted_iota(jnp.int32, (tq, tq), 0))

        def scores(h):
            return [_dot_nt(q[:, h * dv + c * dh:h * dv + (c + 1) * dh],
                            kj[:, h * dv + c * dh:h * dv + (c + 1) * dh]) for c in range(2)]

        nxt = scores(0)
        for h in range(heads):
            cur = nxt
            if h + 1 < heads:
                nxt = scores(h + 1)
            ps = []
            corrs = []
            for c in range(2):
                s = cur[c]
                if on_diagonal:
                    s = jnp.where(visible, s, NEG_BIG)
                m_prev = m_ref[2 * h + c]
                m_next = jnp.maximum(m_prev, jnp.max(s, axis=1, keepdims=True))
                corr = jnp.exp(m_prev - m_next)
                p = jnp.exp(s - jnp.concatenate([m_next] * (tq // LANES), axis=1))
                fold = p[:, :LANES]
                for g in range(1, tq // LANES):
                    fold = fold + p[:, g * LANES:(g + 1) * LANES]
                l_ref[2 * h + c] = l_ref[2 * h + c] * corr + fold
                m_ref[2 * h + c] = m_next
                ps.append(p)
                corrs.append(corr)
            pv = _dot(jnp.concatenate(ps, axis=0).astype(BF16), vj[:, h * dv:(h + 1) * dv])
            acc_ref[2 * h] = acc_ref[2 * h] * corrs[0] + pv[:tq]
            acc_ref[2 * h + 1] = acc_ref[2 * h + 1] * corrs[1] + pv[tq:]

    def off_diagonal(j, carry):
        kv_step(j, False)
        return carry

    lax.fori_loop(0, i, off_diagonal, 0)
    kv_step(i, True)

    lp = lam_ref[...]
    lam = (jnp.exp(jnp.sum(lp[0:1] * lp[1:2], axis=1, keepdims=True))
           - jnp.exp(jnp.sum(lp[2:3] * lp[3:4], axis=1, keepdims=True)) + lam_init)
    for h in range(heads):
        l1 = jnp.sum(l_ref[2 * h], axis=1, keepdims=True)
        l2 = jnp.sum(l_ref[2 * h + 1], axis=1, keepdims=True)
        o = acc_ref[2 * h] / l1 - lam * (acc_ref[2 * h + 1] / l2)
        o = o * lax.rsqrt(jnp.mean(o * o, axis=-1, keepdims=True) + LN_EPS)
        o_ref[:, h * dv:(h + 1) * dv] = (o * g_ref[...] * (1.0 - lam_init)).astype(o_ref.dtype)


def _diff_attention_core(q, k, v, lam_p, subln_g, *, batch, seq, lam_init, tq, heads):
    dv = v.shape[1] // DIFF_HEADS
    q_tiles = seq // tq
    width = heads * dv
    stat = pltpu.VMEM((2 * heads, tq, dv), F32)
    return pl.pallas_call(
        functools.partial(_diff_attn_kernel, tq=tq, lam_init=lam_init, heads=heads),
        grid=(batch, DIFF_HEADS // heads, q_tiles),
        in_specs=[pl.BlockSpec((tq, width), lambda b, hh, i: (b * q_tiles + i, hh)),
                  pl.BlockSpec((seq, width), lambda b, hh, i: (b, hh)),
                  pl.BlockSpec((seq, width), lambda b, hh, i: (b, hh)),
                  pl.BlockSpec(lam_p.shape, lambda b, hh, i: (0, 0)),
                  pl.BlockSpec((1, dv), lambda b, hh, i: (0, 0))],
        out_specs=pl.BlockSpec((tq, width), lambda b, hh, i: (b * q_tiles + i, hh)),
        out_shape=jax.ShapeDtypeStruct(q.shape, BF16),
        scratch_shapes=[stat] * 3,
        compiler_params=_params(("parallel", "parallel", "arbitrary")),
        name="diff_attention_core",
    )(q, k, v, lam_p, subln_g.reshape(1, dv))


def _descending_top(s, count):
    tops = []
    cur = s
    for _ in range(count):
        m = jnp.max(cur, axis=0, keepdims=True)
        tops.append(m)
        cur = jnp.where(cur >= m, NEG_BIG, cur)
    return tops


def _peer_route_kernel(x_ref, wq_ref, keys_ref, xt_ref, e2_ref, thr_ref, coef_ref):
    k = PEER_TOPK
    x = x_ref[...]
    xt_ref[0] = x.T
    qt = _dot_nt(wq_ref[...], x).astype(BF16)
    half = keys_ref.shape[2]
    for h in range(PEER_HEADS):
        s1 = _dot(keys_ref[2 * h], qt[(2 * h) * half:(2 * h + 1) * half, :])
        s2 = _dot(keys_ref[2 * h + 1], qt[(2 * h + 1) * half:(2 * h + 2) * half, :])
        a = _descending_top(s1, k)
        b = _descending_top(s2, k)
        a_all = jnp.concatenate(a, axis=0)
        b_all = jnp.concatenate(b, axis=0)
        cands = [a[0] + b_all]
        cands += [a[i] + b_all[:k // 2] for i in range(1, k // 2)]
        cands += [a_all[k // 2:] + b[0]]
        cand = jnp.concatenate(cands, axis=0)
        tau = _descending_top(cand, k)[k - 1]
        top = a[0] + b[0]
        z = jnp.sum(jnp.where(cand >= tau, jnp.exp(cand - top), 0.0), axis=0, keepdims=True)
        e2_ref[0, h] = jnp.exp(s2 - b[0])
        thr_ref[0, h] = jnp.exp((tau - b[0]) - s1)
        coef_ref[0, h] = jnp.exp(s1 - a[0]) / z


def _peer_route(x_bf, wq_t, keys, *, tm):
    t, d = x_bf.shape
    nk = keys.shape[1]
    heads = PEER_HEADS
    tok = pl.BlockSpec((1, heads, nk, tm), lambda i: (i, 0, 0, 0))
    tok_f32 = jax.ShapeDtypeStruct((t // tm, heads, nk, tm), F32)
    return pl.pallas_call(
        _peer_route_kernel,
        grid=(t // tm,),
        in_specs=[pl.BlockSpec((tm, d), lambda i: (i, 0)),
                  pl.BlockSpec(wq_t.shape, lambda i: (0, 0)),
                  pl.BlockSpec(keys.shape, lambda i: (0, 0, 0))],
        out_specs=[pl.BlockSpec((1, d, tm), lambda i: (i, 0, 0)), tok, tok, tok],
        out_shape=[jax.ShapeDtypeStruct((t // tm, d, tm), BF16), tok_f32, tok_f32, tok_f32],
        compiler_params=_params(("parallel",)),
        name="peer_route",
    )(x_bf, wq_t, keys)


def _gelu_tanh(x):
    inner = x * (0.7978845608028654 + 0.035677408136300125 * (x * x))
    hx = 0.5 * x
    return hx + hx * jnp.tanh(inner)


HALF_ROWS = 64


def _peer_expert_kernel(xt_ref, u_ref, vt_ref, e2_ref, thr_ref, coef_ref, x_ref, g_ref, b_ref,
                        o_ref, obf_ref, acc_ref, pt_ref, *ht_refs, alpha, sub, group):
    j = pl.program_id(1)
    nk = e2_ref.shape[2]
    tm = xt_ref.shape[2]
    te = u_ref.shape[0]
    n_sub = te // sub
    a_per_sub = sub // nk
    ahead = 2

    @pl.when(j == 0)
    def _():
        acc_ref[...] = jnp.zeros_like(acc_ref)

    def first_matmul(sb):
        ht_refs[sb][...] = _dot(u_ref[sb * sub:(sb + 1) * sub, :], xt_ref[0])

    for sb in range(min(ahead, n_sub)):
        first_matmul(sb)
    for sb in range(n_sub):
        ht_ref = ht_refs[sb]
        if sb + ahead < n_sub:
            first_matmul(sb + ahead)
        for lc in range(tm // LANES):
            lanes = slice(lc * LANES, (lc + 1) * LANES)
            for hb in range(nk // HALF_ROWS):
                keys = slice(hb * HALF_ROWS, (hb + 1) * HALF_ROWS)
                ws = [jnp.zeros((HALF_ROWS, LANES), F32)] * a_per_sub
                for h in range(PEER_HEADS):
                    e2 = e2_ref[0, h, keys, lanes]
                    for al in range(a_per_sub):
                        a = sb * a_per_sub + al
                        thr = thr_ref[0, h, a:a + 1, lanes]
                        coef = coef_ref[0, h, a:a + 1, lanes]
                        ws[al] = ws[al] + jnp.where(e2 >= thr, e2, 0.0) * coef
                for al in range(a_per_sub):
                    rows = slice(al * nk + hb * HALF_ROWS, al * nk + (hb + 1) * HALF_ROWS)
                    gated = (ws[al] * _gelu_tanh(ht_ref[rows, lanes])).astype(BF16)
                    pt_ref[sb * sub + rows.start:sb * sub + rows.stop, lanes] = gated
        if (sb + 1) % group == 0:
            grp = slice((sb + 1 - group) * sub, (sb + 1) * sub)
            acc_ref[...] += _dot(vt_ref[0, :, grp], pt_ref[grp, :])

    @pl.when(j == pl.num_programs(1) - 1)
    def _():
        out = _layer_norm_rows(alpha * x_ref[...] + acc_ref[...].T, g_ref[...], b_ref[...])
        o_ref[...] = out
        obf_ref[...] = out.astype(BF16)


def _peer_experts(xt, u, v, e2, thr, coef, x, g, b, *, alpha, te, sub, group):
    n_tiles, d, tm = xt.shape
    t = n_tiles * tm
    n_exp = u.shape[0]
    _, heads, nk, _ = e2.shape
    vt = v.reshape(n_exp // te, te, d).transpose(0, 2, 1)
    row = pl.BlockSpec((tm, d), lambda i, j: (i, 0))
    vec = pl.BlockSpec((1, d), lambda i, j: (0, 0))
    tok = pl.BlockSpec((1, heads, nk, tm), lambda i, j: (i, 0, 0, 0))
    blk = pl.BlockSpec((1, heads, te // nk, tm), lambda i, j: (i, 0, j, 0))
    return pl.pallas_call(
        functools.partial(_peer_expert_kernel, alpha=alpha, sub=sub, group=group),
        grid=(t // tm, n_exp // te),
        in_specs=[pl.BlockSpec((1, d, tm), lambda i, j: (i, 0, 0)),
                  pl.BlockSpec((te, d), lambda i, j: (j, 0)),
                  pl.BlockSpec((1, d, te), lambda i, j: (j, 0, 0)),
                  tok, blk, blk, row, vec, vec],
        out_specs=[row, row],
        out_shape=[jax.ShapeDtypeStruct((t, d), F32), jax.ShapeDtypeStruct((t, d), BF16)],
        scratch_shapes=([pltpu.VMEM((d, tm), F32), pltpu.VMEM((te, tm), BF16)]
                        + [pltpu.VMEM((sub, tm), F32)] * (te // sub)),
        compiler_params=_params(("parallel", "arbitrary")),
        name="peer_experts",
    )(xt, u, vt, e2, thr, coef, x, g.reshape(1, d), b.reshape(1, d))


def _tiles(batch, seq):
    t = batch * seq
    return dict(
        mm_tm=min(seq, 2048), mm_tn=1024,
        ln_tm=min(t, 1024),
        attn_tq=min(seq, 1024), attn_heads=2,
        peer_tm=min(t, 512), exp_te=2048, exp_sub=512, exp_group=2,
    )


def _rope_tables_full(seq, dim, theta):
    pos = jnp.arange(seq, dtype=F32)
    freqs = 1.0 / (theta ** jnp.linspace(0.0, 1.0, dim // 2, dtype=F32))
    ang = pos[:, None] * freqs[None, :]
    return jnp.cos(ang), jnp.sin(ang)


def _rope_tables_partial(seq, head_dim, rope_dim, theta):
    pos = jnp.arange(seq, dtype=F32)
    freqs = theta ** (-jnp.arange(0, rope_dim, 2, dtype=F32) / rope_dim)
    ang = pos[:, None] * freqs[None, :]
    r2 = rope_dim // 2
    lane = jnp.arange(LANES) % head_dim
    cos = jnp.take(jnp.cos(ang), lane % r2, axis=1)
    sin = jnp.take(jnp.sin(ang), lane % r2, axis=1)
    first = (lane < r2)[None, :]
    second = ((lane >= r2) & (lane < rope_dim))[None, :]
    c = jnp.where(first | second, cos, 1.0)
    s_next = jnp.where(first, -sin, 0.0)
    s_prev = jnp.where(second, sin, 0.0)
    return c, s_next, s_prev, r2


def kernel(x, ret_w_in, ret_w_out, kv_w, diff_w_q, diff_lambda, diff_subln_g, diff_w_out,
           peer_w_q, peer_subkeys, peer_u, peer_v, ln_g, ln_b):
    batch, seq, d = x.shape
    t = batch * seq
    depth = peer_w_q.shape[0]
    n_a = ret_w_in.shape[0]
    alpha = (2 * depth) ** 0.25
    tl = _tiles(batch, seq)

    ret_dk = d // RET_HEADS
    hq = RET_HEADS * ret_dk
    diff_dh = d // (2 * DIFF_HEADS)
    kw = DIFF_HEADS * 2 * diff_dh
    ret_cos, ret_sin = _rope_tables_full(seq, ret_dk, RET_THETA)
    dc, ds_next, ds_prev, rot = _rope_tables_partial(seq, diff_dh, diff_dh // 4, ROPE_THETA)

    xf = x.reshape(t, d)
    xb = xf.astype(BF16)
    k_sh = v_sh = None
    for l in range(depth):
        if l < n_a:
            w_in = ret_w_in[l].astype(BF16)
            qk = _matmul_rope_full(xb, w_in[:, :2 * hq], ret_cos, ret_sin, seq=seq, head_dim=ret_dk,
                                   n_unscaled_cols=hq, scale=ret_dk ** -0.5,
                                   tm=tl["mm_tm"], tn=tl["mm_tn"])
            vg = _matmul(xb, w_in[:, 2 * hq:], tm=tl["mm_tm"], tn=tl["mm_tn"])
            y = _retention_core(qk, vg, batch=batch, seq=seq)
            w_out = ret_w_out[l].astype(BF16)
        else:
            j = l - n_a
            q = _matmul_rope_partial(xb, diff_w_q[j].astype(BF16), (dc, ds_next, ds_prev), seq=seq, rot=rot,
                                     scale=diff_dh ** -0.5, tm=tl["mm_tm"], tn=tl["mm_tn"])
            lam_init = 0.8 - 0.6 * math.exp(-0.3 * l)
            y = _diff_attention_core(q, k_sh, v_sh, diff_lambda[j], diff_subln_g[j], batch=batch, seq=seq,
                                     lam_init=lam_init, tq=tl["attn_tq"], heads=tl["attn_heads"])
            w_out = diff_w_out[j].astype(BF16)
        xf, xb = _matmul_residual_ln(y, w_out, xf, ln_g[l, 0], ln_b[l, 0], alpha=alpha, tm=tl["ln_tm"])

        n_keys = peer_subkeys.shape[3]
        keys = peer_subkeys[l].reshape(2 * PEER_HEADS, n_keys, -1).astype(BF16)
        xt, e2, thr, coef = _peer_route(xb, peer_w_q[l].T.astype(BF16), keys, tm=tl["peer_tm"])
        xf, xb = _peer_experts(xt, peer_u[l].astype(BF16), peer_v[l].astype(BF16), e2, thr, coef,
                               xf, ln_g[l, 1], ln_b[l, 1], alpha=alpha,
                               te=tl["exp_te"], sub=tl["exp_sub"], group=tl["exp_group"])
        if l == n_a - 1:
            kv_wb = kv_w.astype(BF16)
            k_sh = _matmul_rope_partial(xb, kv_wb[:, :kw], (dc, ds_next, ds_prev), seq=seq, rot=rot,
                                        scale=1.0, tm=tl["mm_tm"], tn=tl["mm_tn"])
            v_sh = _matmul(xb, kv_wb[:, kw:], tm=tl["mm_tm"], tn=tl["mm_tn"])
    return xf.reshape(batch, seq, d)
```

```python
import functools
import math

import jax
import jax.numpy as jnp
from jax import lax
from jax.experimental import pallas as pl
from jax.experimental.pallas import tpu as pltpu

RET_HEADS = 4
RET_CHUNK = 128
RET_THETA = 10000.0
DIFF_HEADS = 8
ROPE_THETA = 500000.0
PEER_HEADS = 8
PEER_N_KEYS = 128
PEER_TOPK = 16
LN_EPS = 1e-5

LANES = 128
VMEM_LIMIT = 56 * 1024 * 1024
NEG_BIG = -1e30

F32 = jnp.float32
BF16 = jnp.bfloat16


def _params(sem, vmem=VMEM_LIMIT, flags=None):
    return pltpu.CompilerParams(dimension_semantics=sem, vmem_limit_bytes=vmem, flags=flags)


def _dot(a, b):
    return jnp.dot(a, b, preferred_element_type=F32)


def _dot_nt(a, b):
    return lax.dot_general(a, b, (((1,), (1,)), ((), ())), preferred_element_type=F32)


def _dot_tn(a, b):
    return lax.dot_general(a, b, (((0,), (0,)), ((), ())), preferred_element_type=F32)


def _mm_kernel(x_ref, w_ref, o_ref):
    o_ref[...] = _dot(x_ref[...], w_ref[...]).astype(o_ref.dtype)


def _matmul(x, w, *, tm, tn, out_dtype=BF16):
    t, k = x.shape
    n = w.shape[1]
    return pl.pallas_call(
        _mm_kernel,
        grid=(t // tm, n // tn),
        in_specs=[pl.BlockSpec((tm, k), lambda i, j: (i, 0)),
                  pl.BlockSpec((k, tn), lambda i, j: (0, j))],
        out_specs=pl.BlockSpec((tm, tn), lambda i, j: (i, j)),
        out_shape=jax.ShapeDtypeStruct((t, n), out_dtype),
        compiler_params=_params(("parallel", "arbitrary")),
        name="matmul",
    )(x, w)


def _mm_rope_full_kernel(x_ref, w_ref, cos_ref, sin_ref, o_ref, *, head_dim, n_unscaled_tiles, scale):
    acc = _dot(x_ref[...], w_ref[...])
    cos = cos_ref[...]
    sin = sin_ref[...]
    half = head_dim // 2
    s = jnp.where(pl.program_id(1) >= n_unscaled_tiles, scale, 1.0).astype(F32)
    for h in range(acc.shape[1] // head_dim):
        x1 = acc[:, h * head_dim:h * head_dim + half]
        x2 = acc[:, h * head_dim + half:(h + 1) * head_dim]
        o_ref[:, h * head_dim:h * head_dim + half] = ((x1 * cos - x2 * sin) * s).astype(o_ref.dtype)
        o_ref[:, h * head_dim + half:(h + 1) * head_dim] = ((x1 * sin + x2 * cos) * s).astype(o_ref.dtype)


def _matmul_rope_full(x, w, cos, sin, *, seq, head_dim, n_unscaled_cols, scale, tm, tn):
    t, k = x.shape
    n = w.shape[1]
    pos_tiles = seq // tm
    kern = functools.partial(_mm_rope_full_kernel, head_dim=head_dim,
                             n_unscaled_tiles=n_unscaled_cols // tn, scale=scale)
    return pl.pallas_call(
        kern,
        grid=(t // tm, n // tn),
        in_specs=[pl.BlockSpec((tm, k), lambda i, j: (i, 0)),
                  pl.BlockSpec((k, tn), lambda i, j: (0, j)),
                  pl.BlockSpec((tm, head_dim // 2), lambda i, j: (i % pos_tiles, 0)),
                  pl.BlockSpec((tm, head_dim // 2), lambda i, j: (i % pos_tiles, 0))],
        out_specs=pl.BlockSpec((tm, tn), lambda i, j: (i, j)),
        out_shape=jax.ShapeDtypeStruct((t, n), BF16),
        compiler_params=_params(("parallel", "arbitrary")),
        name="matmul_rope_full",
    )(x, w, cos, sin)


def _mm_rope_partial_kernel(x_ref, w_ref, c_ref, s_next_ref, s_prev_ref, o_ref, *, rot, scale):
    acc = _dot(x_ref[...], w_ref[...])
    c = c_ref[...]
    s_next = s_next_ref[...]
    s_prev = s_prev_ref[...]
    for g in range(acc.shape[1] // LANES):
        xg = acc[:, g * LANES:(g + 1) * LANES]
        nxt = pltpu.roll(xg, LANES - rot, axis=1)
        prv = pltpu.roll(xg, rot, axis=1)
        o_ref[:, g * LANES:(g + 1) * LANES] = ((xg * c + nxt * s_next + prv * s_prev) * scale).astype(o_ref.dtype)


def _matmul_rope_partial(x, w, tables, *, seq, rot, scale, tm, tn):
    t, k = x.shape
    n = w.shape[1]
    pos_tiles = seq // tm
    kern = functools.partial(_mm_rope_partial_kernel, rot=rot, scale=scale)
    tab_spec = pl.BlockSpec((tm, LANES), lambda i, j: (i % pos_tiles, 0))
    return pl.pallas_call(
        kern,
        grid=(t // tm, n // tn),
        in_specs=[pl.BlockSpec((tm, k), lambda i, j: (i, 0)),
                  pl.BlockSpec((k, tn), lambda i, j: (0, j)),
                  tab_spec, tab_spec, tab_spec],
        out_specs=pl.BlockSpec((tm, tn), lambda i, j: (i, j)),
        out_shape=jax.ShapeDtypeStruct((t, n), BF16),
        compiler_params=_params(("parallel", "arbitrary")),
        name="matmul_rope_partial",
    )(x, w, *tables)


def _layer_norm_rows(z, g, b):
    mu = jnp.mean(z, axis=-1, keepdims=True)
    zc = z - mu
    var = jnp.mean(zc * zc, axis=-1, keepdims=True)
    return zc * lax.rsqrt(var + LN_EPS) * g + b


def _mm_res_ln_kernel(y_ref, w_ref, x_ref, g_ref, b_ref, o_ref, obf_ref, *, alpha):
    mix = _dot(y_ref[...], w_ref[...])
    out = _layer_norm_rows(alpha * x_ref[...] + mix, g_ref[...], b_ref[...])
    o_ref[...] = out
    obf_ref[...] = out.astype(BF16)


def _matmul_residual_ln(y, w, x, g, b, *, alpha, tm):
    t, k = y.shape
    d = w.shape[1]
    row = pl.BlockSpec((tm, d), lambda i: (i, 0))
    vec = pl.BlockSpec((1, d), lambda i: (0, 0))
    return pl.pallas_call(
        functools.partial(_mm_res_ln_kernel, alpha=alpha),
        grid=(t // tm,),
        in_specs=[pl.BlockSpec((tm, k), lambda i: (i, 0)),
                  pl.BlockSpec((k, d), lambda i: (0, 0)),
                  row, vec, vec],
        out_specs=[row, row],
        out_shape=[jax.ShapeDtypeStruct((t, d), F32), jax.ShapeDtypeStruct((t, d), BF16)],
        compiler_params=_params(("parallel",)),
        name="matmul_residual_ln",
    )(y, w, x, g.reshape(1, d), b.reshape(1, d))


def _retention_kernel(q_ref, k_ref, v_ref, g_ref, dmask_ref, qdec_ref, kdec_ref, cdec_ref,
                      o_ref, state_ref, *, n_chunks):
    c = RET_CHUNK
    state_ref[...] = jnp.zeros_like(state_ref)
    dmask = dmask_ref[0]
    qdec = qdec_ref[0]
    kdec = kdec_ref[0]
    cdec = cdec_ref[0]

    def chunk(ci, carry):
        rows = pl.ds(pl.multiple_of(ci * c, c), c)
        qi = q_ref[rows, :]
        ki = k_ref[rows, :]
        vi = v_ref[rows, :]
        state = state_ref[...]
        sc = _dot_nt(qi, ki) * dmask
        intra = _dot(sc.astype(BF16), vi)
        inter = _dot(qi, state.astype(BF16)) * qdec
        kd = (ki.astype(F32) * kdec).astype(BF16)
        state_ref[...] = state * cdec + _dot_tn(kd, vi)
        y = intra + inter
        mu = jnp.mean(y, axis=-1, keepdims=True)
        yc = y - mu
        var = jnp.mean(yc * yc, axis=-1, keepdims=True)
        yn = yc * lax.rsqrt(var + LN_EPS)
        gate = g_ref[rows, :].astype(F32)
        gate = gate * jax.nn.sigmoid(gate)
        o_ref[rows, :] = (gate * yn).astype(o_ref.dtype)
        return carry

    lax.fori_loop(0, n_chunks, chunk, 0)


def _retention_core(qk, vg, *, batch, seq):
    h = RET_HEADS
    dk = qk.shape[1] // (2 * h)
    dv = vg.shape[1] // (2 * h)
    c = RET_CHUNK
    log_g = jnp.log(1.0 - jnp.exp2(-5.0 - jnp.arange(h, dtype=F32)))
    ar = jnp.arange(c, dtype=F32)
    rel = ar[:, None] - ar[None, :]
    dmask = jnp.where(rel[None] >= 0, jnp.exp(jnp.maximum(rel, 0.0)[None] * log_g[:, None, None]), 0.0)
    qdec = jnp.broadcast_to(jnp.exp((ar + 1.0)[None] * log_g[:, None])[:, :, None], (h, c, dv))
    kdec = jnp.broadcast_to(jnp.exp((c - 1.0 - ar)[None] * log_g[:, None])[:, :, None], (h, c, dk))
    cdec = jnp.broadcast_to(jnp.exp(c * log_g)[:, None, None], (h, 1, dv))
    head_tab = lambda shape: pl.BlockSpec((1,) + shape, lambda b, hh: (hh, 0, 0))
    return pl.pallas_call(
        functools.partial(_retention_kernel, n_chunks=seq // c),
        grid=(batch, h),
        in_specs=[pl.BlockSpec((seq, dk), lambda b, hh: (b, hh)),
                  pl.BlockSpec((seq, dk), lambda b, hh: (b, h + hh)),
                  pl.BlockSpec((seq, dv), lambda b, hh: (b, hh)),
                  pl.BlockSpec((seq, dv), lambda b, hh: (b, h + hh)),
                  head_tab((c, c)), head_tab((c, dv)), head_tab((c, dk)), head_tab((1, dv))],
        out_specs=pl.BlockSpec((seq, dv), lambda b, hh: (b, hh)),
        out_shape=jax.ShapeDtypeStruct((batch * seq, h * dv), BF16),
        scratch_shapes=[pltpu.VMEM((dk, dv), F32)],
        compiler_params=_params(("parallel", "arbitrary")),
        name="retention_core",
    )(qk, qk, vg, vg, dmask, qdec, kdec, cdec)


def _diff_attn_kernel(q_ref, k_ref, v_ref, lam_ref, g_ref, o_ref, m_ref, l_ref, acc_ref,
                      *, tq, lam_init, heads):
    i = pl.program_id(2)
    dv = q_ref.shape[1] // heads
    dh = dv // 2
    m_ref[...] = jnp.full_like(m_ref, NEG_BIG)
    l_ref[...] = jnp.zeros_like(l_ref)
    acc_ref[...] = jnp.zeros_like(acc_ref)
    q = q_ref[...]

    def kv_step(j, on_diagonal):
        rows = pl.ds(pl.multiple_of(j * tq, tq), tq)
        kj = k_ref[rows, :]
        vj = v_ref[rows, :]
        if on_diagonal:
            visible = (lax.broadcasted_iota(jnp.int32, (tq, tq), 1)
                       <= lax.broadcasted_iota(jnp.int32, (tq, tq), 0))

        def scores(h):
            return [_dot_nt(q[:, h * dv + c * dh:h * dv + (c + 1) * dh],
                            kj[:, h * dv + c * dh:h * dv + (c + 1) * dh]) for c in range(2)]

        nxt = scores(0)
        for h in range(heads):
            cur = nxt
            if h + 1 < heads:
                nxt = scores(h + 1)
            ps = []
            corrs = []
            for c in range(2):
                s = cur[c]
                if on_diagonal:
                    s = jnp.where(visible, s, NEG_BIG)
                m_prev = m_ref[2 * h + c]
                m_next = jnp.maximum(m_prev, jnp.max(s, axis=1, keepdims=True))
                corr = jnp.exp(m_prev - m_next)
                p = jnp.exp(s - jnp.concatenate([m_next] * (tq // LANES), axis=1))
                fold = p[:, :LANES]
                for g in range(1, tq // LANES):
                    fold = fold + p[:, g * LANES:(g + 1) * LANES]
                l_ref[2 * h + c] = l_ref[2 * h + c] * corr + fold
                m_ref[2 * h + c] = m_next
                ps.append(p)
                corrs.append(corr)
            pv = _dot(jnp.concatenate(ps, axis=0).astype(BF16), vj[:, h * dv:(h + 1) * dv])
            acc_ref[2 * h] = acc_ref[2 * h] * corrs[0] + pv[:tq]
            acc_ref[2 * h + 1] = acc_ref[2 * h + 1] * corrs[1] + pv[tq:]

    def off_diagonal(j, carry):
        kv_step(j, False)
        return carry

    lax.fori_loop(0, i, off_diagonal, 0)
    kv_step(i, True)

    lp = lam_ref[...]
    lam = (jnp.exp(jnp.sum(lp[0:1] * lp[1:2], axis=1, keepdims=True))
           - jnp.exp(jnp.sum(lp[2:3] * lp[3:4], axis=1, keepdims=True)) + lam_init)
    for h in range(heads):
        l1 = jnp.sum(l_ref[2 * h], axis=1, keepdims=True)
        l2 = jnp.sum(l_ref[2 * h + 1], axis=1, keepdims=True)
        o = acc_ref[2 * h] / l1 - lam * (acc_ref[2 * h + 1] / l2)
        o = o * lax.rsqrt(jnp.mean(o * o, axis=-1, keepdims=True) + LN_EPS)
        o_ref[:, h * dv:(h + 1) * dv] = (o * g_ref[...] * (1.0 - lam_init)).astype(o_ref.dtype)


def _diff_attention_core(q, k, v, lam_p, subln_g, *, batch, seq, lam_init, tq, heads):
    dv = v.shape[1] // DIFF_HEADS
    q_tiles = seq // tq
    width = heads * dv
    stat = pltpu.VMEM((2 * heads, tq, dv), F32)
    return pl.pallas_call(
        functools.partial(_diff_attn_kernel, tq=tq, lam_init=lam_init, heads=heads),
        grid=(batch, DIFF_HEADS // heads, q_tiles),
        in_specs=[pl.BlockSpec((tq, width), lambda b, hh, i: (b * q_tiles + i, hh)),
                  pl.BlockSpec((seq, width), lambda b, hh, i: (b, hh)),
                  pl.BlockSpec((seq, width), lambda b, hh, i: (b, hh)),
                  pl.BlockSpec(lam_p.shape, lambda b, hh, i: (0, 0)),
                  pl.BlockSpec((1, dv), lambda b, hh, i: (0, 0))],
        out_specs=pl.BlockSpec((tq, width), lambda b, hh, i: (b * q_tiles + i, hh)),
        out_shape=jax.ShapeDtypeStruct(q.shape, BF16),
        scratch_shapes=[stat] * 3,
        compiler_params=_params(("parallel", "parallel", "arbitrary")),
        name="diff_attention_core",
    )(q, k, v, lam_p, subln_g.reshape(1, dv))


def _descending_top(s, count):
    tops = []
    cur = s
    for _ in range(count):
        m = jnp.max(cur, axis=0, keepdims=True)
        tops.append(m)
        cur = jnp.where(cur >= m, NEG_BIG, cur)
    return tops


SUBLANES = 8


def _oddeven_merge_sort_pairs(n):
    pairs = []
    p = 1
    while p < n:
        k = p
        while k >= 1:
            for j in range(k % p, n - k, 2 * k):
                for i in range(min(k, n - j - k)):
                    if (i + j) // (2 * p) == (i + j + k) // (2 * p):
                        pairs.append((i + j, i + j + k))
            k //= 2
        p *= 2
    return pairs


def _bitonic_merge_pairs(n):
    pairs = []
    k = n // 2
    while k >= 1:
        for i in range(n):
            if i & k == 0:
                pairs.append((i, i + k))
        k //= 2
    return pairs


def _sorted_top(s):
    n = s.shape[0] // SUBLANES
    v = [s[i * SUBLANES:(i + 1) * SUBLANES, :] for i in range(n)]

    def exchange(pairs):
        for i, j in pairs:
            hi = jnp.maximum(v[i], v[j])
            v[j] = jnp.minimum(v[i], v[j])
            v[i] = hi

    exchange(_oddeven_merge_sort_pairs(n))
    shift = SUBLANES // 2
    while shift >= 1:
        other = [pltpu.roll(v[n - 1 - i], SUBLANES - shift, axis=0) for i in range(n)]
        for i in range(n):
            v[i] = jnp.maximum(v[i], other[i])
        exchange(_bitonic_merge_pairs(n))
        shift //= 2
    return [v[i][0:1, :] for i in range(n)]


def _peer_route_kernel(x_ref, wq_ref, keys_ref, xt_ref, e2_ref, thr_ref, coef_ref):
    k = PEER_TOPK
    x = x_ref[...]
    xt_ref[0] = x.T
    qt = _dot_nt(wq_ref[...], x).astype(BF16)
    half = keys_ref.shape[2]
    for h in range(PEER_HEADS):
        s1 = _dot(keys_ref[2 * h], qt[(2 * h) * half:(2 * h + 1) * half, :])
        s2 = _dot(keys_ref[2 * h + 1], qt[(2 * h + 1) * half:(2 * h + 2) * half, :])
        a = _sorted_top(s1)
        b = _sorted_top(s2)
        assert len(a) == k and len(b) == k
        a_all = jnp.concatenate(a, axis=0)
        b_all = jnp.concatenate(b, axis=0)
        cands = [a[0] + b_all]
        cands += [a[i] + b_all[:k // 2] for i in range(1, k // 2)]
        cands += [a_all[k // 2:] + b[0]]
        cand = jnp.concatenate(cands, axis=0)
        tau = _descending_top(cand, k)[k - 1]
        top = a[0] + b[0]
        z = jnp.sum(jnp.where(cand >= tau, jnp.exp(cand - top), 0.0), axis=0, keepdims=True)
        e2_ref[0, h] = jnp.exp(s2 - b[0])
        thr_ref[0, h] = jnp.exp((tau - b[0]) - s1)
        coef_ref[0, h] = jnp.exp(s1 - a[0]) / z


def _peer_route(x_bf, wq_t, keys, *, tm):
    t, d = x_bf.shape
    nk = keys.shape[1]
    heads = PEER_HEADS
    tok = pl.BlockSpec((1, heads, nk, tm), lambda i: (i, 0, 0, 0))
    tok_f32 = jax.ShapeDtypeStruct((t // tm, heads, nk, tm), F32)
    return pl.pallas_call(
        _peer_route_kernel,
        grid=(t // tm,),
        in_specs=[pl.BlockSpec((tm, d), lambda i: (i, 0)),
                  pl.BlockSpec(wq_t.shape, lambda i: (0, 0)),
                  pl.BlockSpec(keys.shape, lambda i: (0, 0, 0))],
        out_specs=[pl.BlockSpec((1, d, tm), lambda i: (i, 0, 0)), tok, tok, tok],
        out_shape=[jax.ShapeDtypeStruct((t // tm, d, tm), BF16), tok_f32, tok_f32, tok_f32],
        compiler_params=_params(("parallel",)),
        name="peer_route",
    )(x_bf, wq_t, keys)


def _gelu_tanh(x):
    inner = x * (0.7978845608028654 + 0.035677408136300125 * (x * x))
    hx = 0.5 * x
    return hx + hx * jnp.tanh(inner)


HALF_ROWS = 32


def _peer_expert_kernel(xt_ref, u_ref, vt_ref, e2_ref, thr_ref, coef_ref, x_ref, g_ref, b_ref,
                        o_ref, obf_ref, acc_ref, pt_ref, *ht_refs, alpha, sub, group):
    j = pl.program_id(1)
    nk = e2_ref.shape[2]
    tm = xt_ref.shape[2]
    te = u_ref.shape[0]
    n_sub = te // sub
    a_per_sub = sub // nk
    ahead = 2

    @pl.when(j == 0)
    def _():
        acc_ref[...] = jnp.zeros_like(acc_ref)

    def first_matmul(sb):
        ht_refs[sb][...] = _dot(u_ref[sb * sub:(sb + 1) * sub, :], xt_ref[0])

    for sb in range(min(ahead, n_sub)):
        first_matmul(sb)
    for sb in range(n_sub):
        ht_ref = ht_refs[sb]
        if sb + ahead < n_sub:
            first_matmul(sb + ahead)
        for lc in range(tm // LANES):
            lanes = slice(lc * LANES, (lc + 1) * LANES)
            for hb in range(nk // HALF_ROWS):
                keys = slice(hb * HALF_ROWS, (hb + 1) * HALF_ROWS)
                ws = [jnp.zeros((HALF_ROWS, LANES), F32)] * a_per_sub
                for h in range(PEER_HEADS):
                    e2 = e2_ref[0, h, keys, lanes]
                    for al in range(a_per_sub):
                        a = sb * a_per_sub + al
                        thr = thr_ref[0, h, a:a + 1, lanes]
                        coef = coef_ref[0, h, a:a + 1, lanes]
                        ws[al] = ws[al] + jnp.where(e2 >= thr, e2, 0.0) * coef
                for al in range(a_per_sub):
                    rows = slice(al * nk + hb * HALF_ROWS, al * nk + (hb + 1) * HALF_ROWS)
                    gated = (ws[al] * _gelu_tanh(ht_ref[rows, lanes])).astype(BF16)
                    pt_ref[sb * sub + rows.start:sb * sub + rows.stop, lanes] = gated
        if (sb + 1) % group == 0:
            grp = slice((sb + 1 - group) * sub, (sb + 1) * sub)
            acc_ref[...] += _dot(vt_ref[0, :, grp], pt_ref[grp, :])

    @pl.when(j == pl.num_programs(1) - 1)
    def _():
        out = _layer_norm_rows(alpha * x_ref[...] + acc_ref[...].T, g_ref[...], b_ref[...])
        o_ref[...] = out
        obf_ref[...] = out.astype(BF16)


def _peer_experts(xt, u, v, e2, thr, coef, x, g, b, *, alpha, te, sub, group):
    n_tiles, d, tm = xt.shape
    t = n_tiles * tm
    n_exp = u.shape[0]
    _, heads, nk, _ = e2.shape
    vt = v.reshape(n_exp // te, te, d).transpose(0, 2, 1)
    row = pl.BlockSpec((tm, d), lambda i, j: (i, 0))
    vec = pl.BlockSpec((1, d), lambda i, j: (0, 0))
    tok = pl.BlockSpec((1, heads, nk, tm), lambda i, j: (i, 0, 0, 0))
    blk = pl.BlockSpec((1, heads, te // nk, tm), lambda i, j: (i, 0, j, 0))
    return pl.pallas_call(
        functools.partial(_peer_expert_kernel, alpha=alpha, sub=sub, group=group),
        grid=(t // tm, n_exp // te),
        in_specs=[pl.BlockSpec((1, d, tm), lambda i, j: (i, 0, 0)),
                  pl.BlockSpec((te, d), lambda i, j: (j, 0)),
                  pl.BlockSpec((1, d, te), lambda i, j: (j, 0, 0)),
                  tok, blk, blk, row, vec, vec],
        out_specs=[row, row],
        out_shape=[jax.ShapeDtypeStruct((t, d), F32), jax.ShapeDtypeStruct((t, d), BF16)],
        scratch_shapes=([pltpu.VMEM((d, tm), F32), pltpu.VMEM((te, tm), BF16)]
                        + [pltpu.VMEM((sub, tm), F32)] * (te // sub)),
        compiler_params=_params(("parallel", "arbitrary")),
        name="peer_experts",
    )(xt, u, vt, e2, thr, coef, x, g.reshape(1, d), b.reshape(1, d))


def _tiles(batch, seq):
    t = batch * seq
    return dict(
        mm_tm=min(seq, 2048), mm_tn=1024,
        ln_tm=min(t, 1024),
        attn_tq=min(seq, 1024), attn_heads=2,
        peer_tm=min(t, 512), exp_te=2048, exp_sub=512, exp_group=2,
    )


def _rope_tables_full(seq, dim, theta):
    pos = jnp.arange(seq, dtype=F32)
    freqs = 1.0 / (theta ** jnp.linspace(0.0, 1.0, dim // 2, dtype=F32))
    ang = pos[:, None] * freqs[None, :]
    return jnp.cos(ang), jnp.sin(ang)


def _rope_tables_partial(seq, head_dim, rope_dim, theta):
    pos = jnp.arange(seq, dtype=F32)
    freqs = theta ** (-jnp.arange(0, rope_dim, 2, dtype=F32) / rope_dim)
    ang = pos[:, None] * freqs[None, :]
    r2 = rope_dim // 2
    lane = jnp.arange(LANES) % head_dim
    cos = jnp.take(jnp.cos(ang), lane % r2, axis=1)
    sin = jnp.take(jnp.sin(ang), lane % r2, axis=1)
    first = (lane < r2)[None, :]
    second = ((lane >= r2) & (lane < rope_dim))[None, :]
    c = jnp.where(first | second, cos, 1.0)
    s_next = jnp.where(first, -sin, 0.0)
    s_prev = jnp.where(second, sin, 0.0)
    return c, s_next, s_prev, r2


def kernel(x, ret_w_in, ret_w_out, kv_w, diff_w_q, diff_lambda, diff_subln_g, diff_w_out,
           peer_w_q, peer_subkeys, peer_u, peer_v, ln_g, ln_b):
    batch, seq, d = x.shape
    t = batch * seq
    depth = peer_w_q.shape[0]
    n_a = ret_w_in.shape[0]
    alpha = (2 * depth) ** 0.25
    tl = _tiles(batch, seq)

    ret_dk = d // RET_HEADS
    hq = RET_HEADS * ret_dk
    diff_dh = d // (2 * DIFF_HEADS)
    kw = DIFF_HEADS * 2 * diff_dh
    ret_cos, ret_sin = _rope_tables_full(seq, ret_dk, RET_THETA)
    dc, ds_next, ds_prev, rot = _rope_tables_partial(seq, diff_dh, diff_dh // 4, ROPE_THETA)

    xf = x.reshape(t, d)
    xb = xf.astype(BF16)
    k_sh = v_sh = None
    for l in range(depth):
        if l < n_a:
            w_in = ret_w_in[l].astype(BF16)
            qk = _matmul_rope_full(xb, w_in[:, :2 * hq], ret_cos, ret_sin, seq=seq, head_dim=ret_dk,
                                   n_unscaled_cols=hq, scale=ret_dk ** -0.5,
                                   tm=tl["mm_tm"], tn=tl["mm_tn"])
            vg = _matmul(xb, w_in[:, 2 * hq:], tm=tl["mm_tm"], tn=tl["mm_tn"])
            y = _retention_core(qk, vg, batch=batch, seq=seq)
            w_out = ret_w_out[l].astype(BF16)
        else:
            j = l - n_a
            q = _matmul_rope_partial(xb, diff_w_q[j].astype(BF16), (dc, ds_next, ds_prev), seq=seq, rot=rot,
                                     scale=diff_dh ** -0.5, tm=tl["mm_tm"], tn=tl["mm_tn"])
            lam_init = 0.8 - 0.6 * math.exp(-0.3 * l)
            y = _diff_attention_core(q, k_sh, v_sh, diff_lambda[j], diff_subln_g[j], batch=batch, seq=seq,
                                     lam_init=lam_init, tq=tl["attn_tq"], heads=tl["attn_heads"])
            w_out = diff_w_out[j].astype(BF16)
        xf, xb = _matmul_residual_ln(y, w_out, xf, ln_g[l, 0], ln_b[l, 0], alpha=alpha, tm=tl["ln_tm"])

        n_keys = peer_subkeys.shape[3]
        keys = peer_subkeys[l].reshape(2 * PEER_HEADS, n_keys, -1).astype(BF16)
        xt, e2, thr, coef = _peer_route(xb, peer_w_q[l].T.astype(BF16), keys, tm=tl["peer_tm"])
        xf, xb = _peer_experts(xt, peer_u[l].astype(BF16), peer_v[l].astype(BF16), e2, thr, coef,
                               xf, ln_g[l, 1], ln_b[l, 1], alpha=alpha,
                               te=tl["exp_te"], sub=tl["exp_sub"], group=tl["exp_group"])
        if l == n_a - 1:
            kv_wb = kv_w.astype(BF16)
            k_sh = _matmul_rope_partial(xb, kv_wb[:, :kw], (dc, ds_next, ds_prev), seq=seq, rot=rot,
                                        scale=1.0, tm=tl["mm_tm"], tn=tl["mm_tn"])
            v_sh = _matmul(xb, kv_wb[:, kw:], tm=tl["mm_tm"], tn=tl["mm_tn"])
    return xf.reshape(batch, seq, d)
```

```python
import functools
import math

import jax
import jax.numpy as jnp
from jax import lax
from jax.experimental import pallas as pl
from jax.experimental.pallas import tpu as pltpu

RET_HEADS = 4
RET_CHUNK = 128
RET_THETA = 10000.0
DIFF_HEADS = 8
ROPE_THETA = 500000.0
PEER_HEADS = 8
PEER_N_KEYS = 128
PEER_TOPK = 16
LN_EPS = 1e-5

LANES = 128
VMEM_LIMIT = 56 * 1024 * 1024
NEG_BIG = -1e30

F32 = jnp.float32
BF16 = jnp.bfloat16


def _params(sem, vmem=VMEM_LIMIT, flags=None):
    return pltpu.CompilerParams(dimension_semantics=sem, vmem_limit_bytes=vmem, flags=flags)


def _dot(a, b):
    return jnp.dot(a, b, preferred_element_type=F32)


def _dot_nt(a, b):
    return lax.dot_general(a, b, (((1,), (1,)), ((), ())), preferred_element_type=F32)


def _dot_tn(a, b):
    return lax.dot_general(a, b, (((0,), (0,)), ((), ())), preferred_element_type=F32)


def _mm_kernel(x_ref, w_ref, o_ref):
    o_ref[...] = _dot(x_ref[...], w_ref[...]).astype(o_ref.dtype)


def _matmul(x, w, *, tm, tn, out_dtype=BF16):
    t, k = x.shape
    n = w.shape[1]
    return pl.pallas_call(
        _mm_kernel,
        grid=(t // tm, n // tn),
        in_specs=[pl.BlockSpec((tm, k), lambda i, j: (i, 0)),
                  pl.BlockSpec((k, tn), lambda i, j: (0, j))],
        out_specs=pl.BlockSpec((tm, tn), lambda i, j: (i, j)),
        out_shape=jax.ShapeDtypeStruct((t, n), out_dtype),
        compiler_params=_params(("parallel", "arbitrary")),
        name="matmul",
    )(x, w)


def _mm_rope_full_kernel(x_ref, w_ref, cos_ref, sin_ref, o_ref, *, head_dim, n_unscaled_tiles, scale):
    acc = _dot(x_ref[...], w_ref[...])
    cos = cos_ref[...]
    sin = sin_ref[...]
    half = head_dim // 2
    s = jnp.where(pl.program_id(1) >= n_unscaled_tiles, scale, 1.0).astype(F32)
    for h in range(acc.shape[1] // head_dim):
        x1 = acc[:, h * head_dim:h * head_dim + half]
        x2 = acc[:, h * head_dim + half:(h + 1) * head_dim]
        o_ref[:, h * head_dim:h * head_dim + half] = ((x1 * cos - x2 * sin) * s).astype(o_ref.dtype)
        o_ref[:, h * head_dim + half:(h + 1) * head_dim] = ((x1 * sin + x2 * cos) * s).astype(o_ref.dtype)


def _matmul_rope_full(x, w, cos, sin, *, seq, head_dim, n_unscaled_cols, scale, tm, tn):
    t, k = x.shape
    n = w.shape[1]
    pos_tiles = seq // tm
    kern = functools.partial(_mm_rope_full_kernel, head_dim=head_dim,
                             n_unscaled_tiles=n_unscaled_cols // tn, scale=scale)
    return pl.pallas_call(
        kern,
        grid=(t // tm, n // tn),
        in_specs=[pl.BlockSpec((tm, k), lambda i, j: (i, 0)),
                  pl.BlockSpec((k, tn), lambda i, j: (0, j)),
                  pl.BlockSpec((tm, head_dim // 2), lambda i, j: (i % pos_tiles, 0)),
                  pl.BlockSpec((tm, head_dim // 2), lambda i, j: (i % pos_tiles, 0))],
        out_specs=pl.BlockSpec((tm, tn), lambda i, j: (i, j)),
        out_shape=jax.ShapeDtypeStruct((t, n), BF16),
        compiler_params=_params(("parallel", "arbitrary")),
        name="matmul_rope_full",
    )(x, w, cos, sin)


def _mm_rope_partial_kernel(x_ref, w_ref, c_ref, s_next_ref, s_prev_ref, o_ref, *, rot, scale):
    acc = _dot(x_ref[...], w_ref[...])
    c = c_ref[...]
    s_next = s_next_ref[...]
    s_prev = s_prev_ref[...]
    for g in range(acc.shape[1] // LANES):
        xg = acc[:, g * LANES:(g + 1) * LANES]
        nxt = pltpu.roll(xg, LANES - rot, axis=1)
        prv = pltpu.roll(xg, rot, axis=1)
        o_ref[:, g * LANES:(g + 1) * LANES] = ((xg * c + nxt * s_next + prv * s_prev) * scale).astype(o_ref.dtype)


def _matmul_rope_partial(x, w, tables, *, seq, rot, scale, tm, tn):
    t, k = x.shape
    n = w.shape[1]
    pos_tiles = seq // tm
    kern = functools.partial(_mm_rope_partial_kernel, rot=rot, scale=scale)
    tab_spec = pl.BlockSpec((tm, LANES), lambda i, j: (i % pos_tiles, 0))
    return pl.pallas_call(
        kern,
        grid=(t // tm, n // tn),
        in_specs=[pl.BlockSpec((tm, k), lambda i, j: (i, 0)),
                  pl.BlockSpec((k, tn), lambda i, j: (0, j)),
                  tab_spec, tab_spec, tab_spec],
        out_specs=pl.BlockSpec((tm, tn), lambda i, j: (i, j)),
        out_shape=jax.ShapeDtypeStruct((t, n), BF16),
        compiler_params=_params(("parallel", "arbitrary")),
        name="matmul_rope_partial",
    )(x, w, *tables)


def _layer_norm_rows(z, g, b):
    mu = jnp.mean(z, axis=-1, keepdims=True)
    zc = z - mu
    var = jnp.mean(zc * zc, axis=-1, keepdims=True)
    return zc * lax.rsqrt(var + LN_EPS) * g + b


def _mm_res_ln_kernel(y_ref, w_ref, x_ref, g_ref, b_ref, o_ref, obf_ref, *, alpha):
    mix = _dot(y_ref[...], w_ref[...])
    out = _layer_norm_rows(alpha * x_ref[...] + mix, g_ref[...], b_ref[...])
    o_ref[...] = out
    obf_ref[...] = out.astype(BF16)


def _matmul_residual_ln(y, w, x, g, b, *, alpha, tm):
    t, k = y.shape
    d = w.shape[1]
    row = pl.BlockSpec((tm, d), lambda i: (i, 0))
    vec = pl.BlockSpec((1, d), lambda i: (0, 0))
    return pl.pallas_call(
        functools.partial(_mm_res_ln_kernel, alpha=alpha),
        grid=(t // tm,),
        in_specs=[pl.BlockSpec((tm, k), lambda i: (i, 0)),
                  pl.BlockSpec((k, d), lambda i: (0, 0)),
                  row, vec, vec],
        out_specs=[row, row],
        out_shape=[jax.ShapeDtypeStruct((t, d), F32), jax.ShapeDtypeStruct((t, d), BF16)],
        compiler_params=_params(("parallel",)),
        name="matmul_residual_ln",
    )(y, w, x, g.reshape(1, d), b.reshape(1, d))


def _retention_kernel(q_ref, k_ref, v_ref, g_ref, dmask_ref, qdec_ref, kdec_ref, cdec_ref,
                      o_ref, state_ref, *, n_chunks):
    c = RET_CHUNK
    heads = state_ref.shape[0]
    dk = q_ref.shape[1] // heads
    dv = v_ref.shape[1] // heads
    state_ref[...] = jnp.zeros_like(state_ref)

    def chunk(ci, carry):
        rows = pl.ds(pl.multiple_of(ci * c, c), c)
        for p in range(heads):
            kcols = slice(p * dk, (p + 1) * dk)
            vcols = slice(p * dv, (p + 1) * dv)
            qi = q_ref[rows, kcols]
            ki = k_ref[rows, kcols]
            vi = v_ref[rows, vcols]
            state = state_ref[p]
            sc = _dot_nt(qi, ki) * dmask_ref[p]
            intra = _dot(sc.astype(BF16), vi)
            inter = _dot(qi, state.astype(BF16)) * qdec_ref[p]
            kd = (ki.astype(F32) * kdec_ref[p]).astype(BF16)
            state_ref[p] = state * cdec_ref[p] + _dot_tn(kd, vi)
            y = intra + inter
            mu = jnp.mean(y, axis=-1, keepdims=True)
            yc = y - mu
            var = jnp.mean(yc * yc, axis=-1, keepdims=True)
            yn = yc * lax.rsqrt(var + LN_EPS)
            gate = g_ref[rows, vcols].astype(F32)
            gate = gate * jax.nn.sigmoid(gate)
            o_ref[rows, vcols] = (gate * yn).astype(o_ref.dtype)
        return carry

    lax.fori_loop(0, n_chunks, chunk, 0)


def _retention_core(qk, vg, *, batch, seq, heads):
    h = RET_HEADS
    dk = qk.shape[1] // (2 * h)
    dv = vg.shape[1] // (2 * h)
    c = RET_CHUNK
    steps = h // heads
    log_g = jnp.log(1.0 - jnp.exp2(-5.0 - jnp.arange(h, dtype=F32)))
    ar = jnp.arange(c, dtype=F32)
    rel = ar[:, None] - ar[None, :]
    dmask = jnp.where(rel[None] >= 0, jnp.exp(jnp.maximum(rel, 0.0)[None] * log_g[:, None, None]), 0.0)
    qdec = jnp.broadcast_to(jnp.exp((ar + 1.0)[None] * log_g[:, None])[:, :, None], (h, c, dv))
    kdec = jnp.broadcast_to(jnp.exp((c - 1.0 - ar)[None] * log_g[:, None])[:, :, None], (h, c, dk))
    cdec = jnp.broadcast_to(jnp.exp(c * log_g)[:, None, None], (h, 1, dv))
    head_tab = lambda shape: pl.BlockSpec((heads,) + shape, lambda b, hh: (hh, 0, 0))
    return pl.pallas_call(
        functools.partial(_retention_kernel, n_chunks=seq // c),
        grid=(batch, steps),
        in_specs=[pl.BlockSpec((seq, heads * dk), lambda b, hh: (b, hh)),
                  pl.BlockSpec((seq, heads * dk), lambda b, hh: (b, steps + hh)),
                  pl.BlockSpec((seq, heads * dv), lambda b, hh: (b, hh)),
                  pl.BlockSpec((seq, heads * dv), lambda b, hh: (b, steps + hh)),
                  head_tab((c, c)), head_tab((c, dv)), head_tab((c, dk)), head_tab((1, dv))],
        out_specs=pl.BlockSpec((seq, heads * dv), lambda b, hh: (b, hh)),
        out_shape=jax.ShapeDtypeStruct((batch * seq, h * dv), BF16),
        scratch_shapes=[pltpu.VMEM((heads, dk, dv), F32)],
        compiler_params=_params(("parallel", "arbitrary")),
        name="retention_core",
    )(qk, qk, vg, vg, dmask, qdec, kdec, cdec)


def _diff_attn_kernel(q_ref, k_ref, v_ref, lam_ref, g_ref, o_ref, m_ref, l_ref, acc_ref,
                      *, tq, lam_init, heads):
    i = pl.program_id(2)
    dv = q_ref.shape[1] // heads
    dh = dv // 2
    m_ref[...] = jnp.full_like(m_ref, NEG_BIG)
    l_ref[...] = jnp.zeros_like(l_ref)
    acc_ref[...] = jnp.zeros_like(acc_ref)
    q = q_ref[...]

    def kv_step(j, on_diagonal):
        rows = pl.ds(pl.multiple_of(j * tq, tq), tq)
        kj = k_ref[rows, :]
        vj = v_ref[rows, :]
        if on_diagonal:
            visible = (lax.broadcasted_iota(jnp.int32, (tq, tq), 1)
                       <= lax.broadcasted_iota(jnp.int32, (tq, tq), 0))

        def scores(h):
            return [_dot_nt(q[:, h * dv + c * dh:h * dv + (c + 1) * dh],
                            kj[:, h * dv + c * dh:h * dv + (c + 1) * dh]) for c in range(2)]

        nxt = scores(0)
        for h in range(heads):
            cur = nxt
            if h + 1 < heads:
                nxt = scores(h + 1)
            ps = []
            corrs = []
            for c in range(2):
                s = cur[c]
                if on_diagonal:
                    s = jnp.where(visible, s, NEG_BIG)
                m_prev = m_ref[2 * h + c]
                m_next = jnp.maximum(m_prev, jnp.max(s, axis=1, keepdims=True))
                corr = jnp.exp(m_prev - m_next)
                p = jnp.exp(s - jnp.concatenate([m_next] * (tq // LANES), axis=1))
                fold = p[:, :LANES]
                for g in range(1, tq // LANES):
                    fold = fold + p[:, g * LANES:(g + 1) * LANES]
                l_ref[2 * h + c] = l_ref[2 * h + c] * corr + fold
                m_ref[2 * h + c] = m_next
                ps.append(p)
                corrs.append(corr)
            pv = _dot(jnp.concatenate(ps, axis=0).astype(BF16), vj[:, h * dv:(h + 1) * dv])
            acc_ref[2 * h] = acc_ref[2 * h] * corrs[0] + pv[:tq]
            acc_ref[2 * h + 1] = acc_ref[2 * h + 1] * corrs[1] + pv[tq:]

    def off_diagonal(j, carry):
        kv_step(j, False)
        return carry

    lax.fori_loop(0, i, off_diagonal, 0)
    kv_step(i, True)

    lp = lam_ref[...]
    lam = (jnp.exp(jnp.sum(lp[0:1] * lp[1:2], axis=1, keepdims=True))
           - jnp.exp(jnp.sum(lp[2:3] * lp[3:4], axis=1, keepdims=True)) + lam_init)
    for h in range(heads):
        l1 = jnp.sum(l_ref[2 * h], axis=1, keepdims=True)
        l2 = jnp.sum(l_ref[2 * h + 1], axis=1, keepdims=True)
        o = acc_ref[2 * h] / l1 - lam * (acc_ref[2 * h + 1] / l2)
        o = o * lax.rsqrt(jnp.mean(o * o, axis=-1, keepdims=True) + LN_EPS)
        o_ref[:, h * dv:(h + 1) * dv] = (o * g_ref[...] * (1.0 - lam_init)).astype(o_ref.dtype)


def _diff_attention_core(q, k, v, lam_p, subln_g, *, batch, seq, lam_init, tq, heads):
    dv = v.shape[1] // DIFF_HEADS
    q_tiles = seq // tq
    width = heads * dv
    stat = pltpu.VMEM((2 * heads, tq, dv), F32)
    return pl.pallas_call(
        functools.partial(_diff_attn_kernel, tq=tq, lam_init=lam_init, heads=heads),
        grid=(batch, DIFF_HEADS // heads, q_tiles),
        in_specs=[pl.BlockSpec((tq, width), lambda b, hh, i: (b * q_tiles + i, hh)),
                  pl.BlockSpec((seq, width), lambda b, hh, i: (b, hh)),
                  pl.BlockSpec((seq, width), lambda b, hh, i: (b, hh)),
                  pl.BlockSpec(lam_p.shape, lambda b, hh, i: (0, 0)),
                  pl.BlockSpec((1, dv), lambda b, hh, i: (0, 0))],
        out_specs=pl.BlockSpec((tq, width), lambda b, hh, i: (b * q_tiles + i, hh)),
        out_shape=jax.ShapeDtypeStruct(q.shape, BF16),
        scratch_shapes=[stat] * 3,
        compiler_params=_params(("parallel", "parallel", "arbitrary")),
        name="diff_attention_core",
    )(q, k, v, lam_p, subln_g.reshape(1, dv))


def _descending_top(s, count):
    tops = []
    cur = s
    for _ in range(count):
        m = jnp.max(cur, axis=0, keepdims=True)
        tops.append(m)
        cur = jnp.where(cur >= m, NEG_BIG, cur)
    return tops


SUBLANES = 8


def _oddeven_merge_sort_pairs(n):
    pairs = []
    p = 1
    while p < n:
        k = p
        while k >= 1:
            for j in range(k % p, n - k, 2 * k):
                for i in range(min(k, n - j - k)):
                    if (i + j) // (2 * p) == (i + j + k) // (2 * p):
                        pairs.append((i + j, i + j + k))
            k //= 2
        p *= 2
    return pairs


def _bitonic_merge_pairs(n):
    pairs = []
    k = n // 2
    while k >= 1:
        for i in range(n):
            if i & k == 0:
                pairs.append((i, i + k))
        k //= 2
    return pairs


def _sorted_top(s):
    n = s.shape[0] // SUBLANES
    v = [s[i * SUBLANES:(i + 1) * SUBLANES, :] for i in range(n)]

    def exchange(pairs):
        for i, j in pairs:
            hi = jnp.maximum(v[i], v[j])
            v[j] = jnp.minimum(v[i], v[j])
            v[i] = hi

    exchange(_oddeven_merge_sort_pairs(n))
    shift = SUBLANES // 2
    while shift >= 1:
        other = [pltpu.roll(v[n - 1 - i], SUBLANES - shift, axis=0) for i in range(n)]
        for i in range(n):
            v[i] = jnp.maximum(v[i], other[i])
        exchange(_bitonic_merge_pairs(n))
        shift //= 2
    return [v[i][0:1, :] for i in range(n)]


def _peer_route_kernel(x_ref, wq_ref, keys_ref, xt_ref, e2_ref, thr_ref, coef_ref):
    k = PEER_TOPK
    x = x_ref[...]
    xt_ref[0] = x.T
    qt = _dot_nt(wq_ref[...], x).astype(BF16)
    half = keys_ref.shape[2]
    for h in range(PEER_HEADS):
        s1 = _dot(keys_ref[2 * h], qt[(2 * h) * half:(2 * h + 1) * half, :])
        s2 = _dot(keys_ref[2 * h + 1], qt[(2 * h + 1) * half:(2 * h + 2) * half, :])
        a = _sorted_top(s1)
        b = _sorted_top(s2)
        assert len(a) == k and len(b) == k
        a_all = jnp.concatenate(a, axis=0)
        b_all = jnp.concatenate(b, axis=0)
        cands = [a[0] + b_all]
        cands += [a[i] + b_all[:k // 2] for i in range(1, k // 2)]
        cands += [a_all[k // 2:] + b[0]]
        cand = jnp.concatenate(cands, axis=0)
        tau = _descending_top(cand, k)[k - 1]
        top = a[0] + b[0]
        z = jnp.sum(jnp.where(cand >= tau, jnp.exp(cand - top), 0.0), axis=0, keepdims=True)
        e2_ref[0, h] = jnp.exp(s2 - b[0])
        thr_ref[0, h] = jnp.exp((tau - b[0]) - s1)
        coef_ref[0, h] = jnp.exp(s1 - a[0]) / z


def _peer_route(x_bf, wq_t, keys, *, tm):
    t, d = x_bf.shape
    nk = keys.shape[1]
    heads = PEER_HEADS
    tok = pl.BlockSpec((1, heads, nk, tm), lambda i: (i, 0, 0, 0))
    tok_f32 = jax.ShapeDtypeStruct((t // tm, heads, nk, tm), F32)
    return pl.pallas_call(
        _peer_route_kernel,
        grid=(t // tm,),
        in_specs=[pl.BlockSpec((tm, d), lambda i: (i, 0)),
                  pl.BlockSpec(wq_t.shape, lambda i: (0, 0)),
                  pl.BlockSpec(keys.shape, lambda i: (0, 0, 0))],
        out_specs=[pl.BlockSpec((1, d, tm), lambda i: (i, 0, 0)), tok, tok, tok],
        out_shape=[jax.ShapeDtypeStruct((t // tm, d, tm), BF16), tok_f32, tok_f32, tok_f32],
        compiler_params=_params(("parallel",)),
        name="peer_route",
    )(x_bf, wq_t, keys)


def _gelu_tanh(x):
    inner = x * (0.7978845608028654 + 0.035677408136300125 * (x * x))
    hx = 0.5 * x
    return hx + hx * jnp.tanh(inner)


HALF_ROWS = 32


def _peer_expert_kernel(xt_ref, u_ref, vt_ref, e2_ref, thr_ref, coef_ref, x_ref, g_ref, b_ref,
                        o_ref, obf_ref, acc_ref, pt_ref, *ht_refs, alpha, sub, group):
    j = pl.program_id(1)
    nk = e2_ref.shape[2]
    tm = xt_ref.shape[2]
    te = u_ref.shape[0]
    n_sub = te // sub
    a_per_sub = sub // nk
    ahead = 2

    @pl.when(j == 0)
    def _():
        acc_ref[...] = jnp.zeros_like(acc_ref)

    def first_matmul(sb):
        ht_refs[sb][...] = _dot(u_ref[sb * sub:(sb + 1) * sub, :], xt_ref[0])

    for sb in range(min(ahead, n_sub)):
        first_matmul(sb)
    for sb in range(n_sub):
        ht_ref = ht_refs[sb]
        if sb + ahead < n_sub:
            first_matmul(sb + ahead)
        for lc in range(tm // LANES):
            lanes = slice(lc * LANES, (lc + 1) * LANES)
            for hb in range(nk // HALF_ROWS):
                keys = slice(hb * HALF_ROWS, (hb + 1) * HALF_ROWS)
                ws = [jnp.zeros((HALF_ROWS, LANES), F32)] * a_per_sub
                for h in range(PEER_HEADS):
                    e2 = e2_ref[0, h, keys, lanes]
                    for al in range(a_per_sub):
                        a = sb * a_per_sub + al
                        thr = thr_ref[0, h, a:a + 1, lanes]
                        coef = coef_ref[0, h, a:a + 1, lanes]
                        ws[al] = ws[al] + jnp.where(e2 >= thr, e2, 0.0) * coef
                for al in range(a_per_sub):
                    rows = slice(al * nk + hb * HALF_ROWS, al * nk + (hb + 1) * HALF_ROWS)
                    gated = (ws[al] * _gelu_tanh(ht_ref[rows, lanes])).astype(BF16)
                    pt_ref[sb * sub + rows.start:sb * sub + rows.stop, lanes] = gated
        if (sb + 1) % group == 0:
            grp = slice((sb + 1 - group) * sub, (sb + 1) * sub)
            acc_ref[...] += _dot(vt_ref[0, :, grp], pt_ref[grp, :])

    @pl.when(j == pl.num_programs(1) - 1)
    def _():
        out = _layer_norm_rows(alpha * x_ref[...] + acc_ref[...].T, g_ref[...], b_ref[...])
        o_ref[...] = out
        obf_ref[...] = out.astype(BF16)


def _peer_experts(xt, u, v, e2, thr, coef, x, g, b, *, alpha, te, sub, group):
    n_tiles, d, tm = xt.shape
    t = n_tiles * tm
    n_exp = u.shape[0]
    _, heads, nk, _ = e2.shape
    vt = v.reshape(n_exp // te, te, d).transpose(0, 2, 1)
    row = pl.BlockSpec((tm, d), lambda i, j: (i, 0))
    vec = pl.BlockSpec((1, d), lambda i, j: (0, 0))
    tok = pl.BlockSpec((1, heads, nk, tm), lambda i, j: (i, 0, 0, 0))
    blk = pl.BlockSpec((1, heads, te // nk, tm), lambda i, j: (i, 0, j, 0))
    return pl.pallas_call(
        functools.partial(_peer_expert_kernel, alpha=alpha, sub=sub, group=group),
        grid=(t // tm, n_exp // te),
        in_specs=[pl.BlockSpec((1, d, tm), lambda i, j: (i, 0, 0)),
                  pl.BlockSpec((te, d), lambda i, j: (j, 0)),
                  pl.BlockSpec((1, d, te), lambda i, j: (j, 0, 0)),
                  tok, blk, blk, row, vec, vec],
        out_specs=[row, row],
        out_shape=[jax.ShapeDtypeStruct((t, d), F32), jax.ShapeDtypeStruct((t, d), BF16)],
        scratch_shapes=([pltpu.VMEM((d, tm), F32), pltpu.VMEM((te, tm), BF16)]
                        + [pltpu.VMEM((sub, tm), F32)] * (te // sub)),
        compiler_params=_params(("parallel", "arbitrary")),
        name="peer_experts",
    )(xt, u, vt, e2, thr, coef, x, g.reshape(1, d), b.reshape(1, d))


def _tiles(batch, seq):
    t = batch * seq
    return dict(
        mm_tm=min(seq, 2048), mm_tn=1024,
        ln_tm=min(t, 1024), ret_heads=2,
        attn_tq=min(seq, 1024), attn_heads=2,
        peer_tm=min(t, 512), exp_te=2048, exp_sub=512, exp_group=2,
    )


def _rope_tables_full(seq, dim, theta):
    pos = jnp.arange(seq, dtype=F32)
    freqs = 1.0 / (theta ** jnp.linspace(0.0, 1.0, dim // 2, dtype=F32))
    ang = pos[:, None] * freqs[None, :]
    return jnp.cos(ang), jnp.sin(ang)


def _rope_tables_partial(seq, head_dim, rope_dim, theta):
    pos = jnp.arange(seq, dtype=F32)
    freqs = theta ** (-jnp.arange(0, rope_dim, 2, dtype=F32) / rope_dim)
    ang = pos[:, None] * freqs[None, :]
    r2 = rope_dim // 2
    lane = jnp.arange(LANES) % head_dim
    cos = jnp.take(jnp.cos(ang), lane % r2, axis=1)
    sin = jnp.take(jnp.sin(ang), lane % r2, axis=1)
    first = (lane < r2)[None, :]
    second = ((lane >= r2) & (lane < rope_dim))[None, :]
    c = jnp.where(first | second, cos, 1.0)
    s_next = jnp.where(first, -sin, 0.0)
    s_prev = jnp.where(second, sin, 0.0)
    return c, s_next, s_prev, r2


def kernel(x, ret_w_in, ret_w_out, kv_w, diff_w_q, diff_lambda, diff_subln_g, diff_w_out,
           peer_w_q, peer_subkeys, peer_u, peer_v, ln_g, ln_b):
    batch, seq, d = x.shape
    t = batch * seq
    depth = peer_w_q.shape[0]
    n_a = ret_w_in.shape[0]
    alpha = (2 * depth) ** 0.25
    tl = _tiles(batch, seq)

    ret_dk = d // RET_HEADS
    hq = RET_HEADS * ret_dk
    diff_dh = d // (2 * DIFF_HEADS)
    kw = DIFF_HEADS * 2 * diff_dh
    ret_cos, ret_sin = _rope_tables_full(seq, ret_dk, RET_THETA)
    dc, ds_next, ds_prev, rot = _rope_tables_partial(seq, diff_dh, diff_dh // 4, ROPE_THETA)

    xf = x.reshape(t, d)
    xb = xf.astype(BF16)
    k_sh = v_sh = None
    for l in range(depth):
        if l < n_a:
            w_in = ret_w_in[l].astype(BF16)
            qk = _matmul_rope_full(xb, w_in[:, :2 * hq], ret_cos, ret_sin, seq=seq, head_dim=ret_dk,
                                   n_unscaled_cols=hq, scale=ret_dk ** -0.5,
                                   tm=tl["mm_tm"], tn=tl["mm_tn"])
            vg = _matmul(xb, w_in[:, 2 * hq:], tm=tl["mm_tm"], tn=tl["mm_tn"])
            y = _retention_core(qk, vg, batch=batch, seq=seq, heads=tl["ret_heads"])
            w_out = ret_w_out[l].astype(BF16)
        else:
            j = l - n_a
            q = _matmul_rope_partial(xb, diff_w_q[j].astype(BF16), (dc, ds_next, ds_prev), seq=seq, rot=rot,
                                     scale=diff_dh ** -0.5, tm=tl["mm_tm"], tn=tl["mm_tn"])
            lam_init = 0.8 - 0.6 * math.exp(-0.3 * l)
            y = _diff_attention_core(q, k_sh, v_sh, diff_lambda[j], diff_subln_g[j], batch=batch, seq=seq,
                                     lam_init=lam_init, tq=tl["attn_tq"], heads=tl["attn_heads"])
            w_out = diff_w_out[j].astype(BF16)
        xf, xb = _matmul_residual_ln(y, w_out, xf, ln_g[l, 0], ln_b[l, 0], alpha=alpha, tm=tl["ln_tm"])

        n_keys = peer_subkeys.shape[3]
        keys = peer_subkeys[l].reshape(2 * PEER_HEADS, n_keys, -1).astype(BF16)
        xt, e2, thr, coef = _peer_route(xb, peer_w_q[l].T.astype(BF16), keys, tm=tl["peer_tm"])
        xf, xb = _peer_experts(xt, peer_u[l].astype(BF16), peer_v[l].astype(BF16), e2, thr, coef,
                               xf, ln_g[l, 1], ln_b[l, 1], alpha=alpha,
                               te=tl["exp_te"], sub=tl["exp_sub"], group=tl["exp_group"])
        if l == n_a - 1:
            kv_wb = kv_w.astype(BF16)
            k_sh = _matmul_rope_partial(xb, kv_wb[:, :kw], (dc, ds_next, ds_prev), seq=seq, rot=rot,
                                        scale=1.0, tm=tl["mm_tm"], tn=tl["mm_tn"])
            v_sh = _matmul(xb, kv_wb[:, kw:], tm=tl["mm_tm"], tn=tl["mm_tn"])
    return xf.reshape(batch, seq, d)
```

```python
import functools
import math

import jax
import jax.numpy as jnp
from jax import lax
from jax.experimental import pallas as pl
from jax.experimental.pallas import tpu as pltpu

RET_HEADS = 4
RET_CHUNK = 128
RET_THETA = 10000.0
DIFF_HEADS = 8
ROPE_THETA = 500000.0
PEER_HEADS = 8
PEER_N_KEYS = 128
PEER_TOPK = 16
LN_EPS = 1e-5

LANES = 128
VMEM_LIMIT = 56 * 1024 * 1024
NEG_BIG = -1e30

F32 = jnp.float32
BF16 = jnp.bfloat16


def _params(sem, vmem=VMEM_LIMIT, flags=None):
    return pltpu.CompilerParams(dimension_semantics=sem, vmem_limit_bytes=vmem, flags=flags)


def _dot(a, b):
    return jnp.dot(a, b, preferred_element_type=F32)


def _dot_nt(a, b):
    return lax.dot_general(a, b, (((1,), (1,)), ((), ())), preferred_element_type=F32)


def _dot_tn(a, b):
    return lax.dot_general(a, b, (((0,), (0,)), ((), ())), preferred_element_type=F32)


def _mm_kernel(x_ref, w_ref, o_ref):
    o_ref[...] = _dot(x_ref[...], w_ref[...]).astype(o_ref.dtype)


def _matmul(x, w, *, tm, tn, out_dtype=BF16):
    t, k = x.shape
    n = w.shape[1]
    return pl.pallas_call(
        _mm_kernel,
        grid=(t // tm, n // tn),
        in_specs=[pl.BlockSpec((tm, k), lambda i, j: (i, 0)),
                  pl.BlockSpec((k, tn), lambda i, j: (0, j))],
        out_specs=pl.BlockSpec((tm, tn), lambda i, j: (i, j)),
        out_shape=jax.ShapeDtypeStruct((t, n), out_dtype),
        compiler_params=_params(("parallel", "arbitrary")),
        name="matmul",
    )(x, w)


def _mm_rope_full_kernel(x_ref, w_ref, cos_ref, sin_ref, o_ref, *, head_dim, n_unscaled_tiles, scale):
    acc = _dot(x_ref[...], w_ref[...])
    cos = cos_ref[...]
    sin = sin_ref[...]
    half = head_dim // 2
    s = jnp.where(pl.program_id(1) >= n_unscaled_tiles, scale, 1.0).astype(F32)
    for h in range(acc.shape[1] // head_dim):
        x1 = acc[:, h * head_dim:h * head_dim + half]
        x2 = acc[:, h * head_dim + half:(h + 1) * head_dim]
        o_ref[:, h * head_dim:h * head_dim + half] = ((x1 * cos - x2 * sin) * s).astype(o_ref.dtype)
        o_ref[:, h * head_dim + half:(h + 1) * head_dim] = ((x1 * sin + x2 * cos) * s).astype(o_ref.dtype)


def _matmul_rope_full(x, w, cos, sin, *, seq, head_dim, n_unscaled_cols, scale, tm, tn):
    t, k = x.shape
    n = w.shape[1]
    pos_tiles = seq // tm
    kern = functools.partial(_mm_rope_full_kernel, head_dim=head_dim,
                             n_unscaled_tiles=n_unscaled_cols // tn, scale=scale)
    return pl.pallas_call(
        kern,
        grid=(t // tm, n // tn),
        in_specs=[pl.BlockSpec((tm, k), lambda i, j: (i, 0)),
                  pl.BlockSpec((k, tn), lambda i, j: (0, j)),
                  pl.BlockSpec((tm, head_dim // 2), lambda i, j: (i % pos_tiles, 0)),
                  pl.BlockSpec((tm, head_dim // 2), lambda i, j: (i % pos_tiles, 0))],
        out_specs=pl.BlockSpec((tm, tn), lambda i, j: (i, j)),
        out_shape=jax.ShapeDtypeStruct((t, n), BF16),
        compiler_params=_params(("parallel", "arbitrary")),
        name="matmul_rope_full",
    )(x, w, cos, sin)


def _mm_rope_partial_kernel(x_ref, w_ref, c_ref, s_next_ref, s_prev_ref, o_ref, *, rot, scale):
    acc = _dot(x_ref[...], w_ref[...])
    c = c_ref[...]
    s_next = s_next_ref[...]
    s_prev = s_prev_ref[...]
    for g in range(acc.shape[1] // LANES):
        xg = acc[:, g * LANES:(g + 1) * LANES]
        nxt = pltpu.roll(xg, LANES - rot, axis=1)
        prv = pltpu.roll(xg, rot, axis=1)
        o_ref[:, g * LANES:(g + 1) * LANES] = ((xg * c + nxt * s_next + prv * s_prev) * scale).astype(o_ref.dtype)


def _matmul_rope_partial(x, w, tables, *, seq, rot, scale, tm, tn):
    t, k = x.shape
    n = w.shape[1]
    pos_tiles = seq // tm
    kern = functools.partial(_mm_rope_partial_kernel, rot=rot, scale=scale)
    tab_spec = pl.BlockSpec((tm, LANES), lambda i, j: (i % pos_tiles, 0))
    return pl.pallas_call(
        kern,
        grid=(t // tm, n // tn),
        in_specs=[pl.BlockSpec((tm, k), lambda i, j: (i, 0)),
                  pl.BlockSpec((k, tn), lambda i, j: (0, j)),
                  tab_spec, tab_spec, tab_spec],
        out_specs=pl.BlockSpec((tm, tn), lambda i, j: (i, j)),
        out_shape=jax.ShapeDtypeStruct((t, n), BF16),
        compiler_params=_params(("parallel", "arbitrary")),
        name="matmul_rope_partial",
    )(x, w, *tables)


def _layer_norm_rows(z, g, b):
    mu = jnp.mean(z, axis=-1, keepdims=True)
    zc = z - mu
    var = jnp.mean(zc * zc, axis=-1, keepdims=True)
    return zc * lax.rsqrt(var + LN_EPS) * g + b


def _mm_res_ln_kernel(y_ref, w_ref, x_ref, g_ref, b_ref, o_ref, obf_ref, *, alpha):
    mix = _dot(y_ref[...], w_ref[...])
    out = _layer_norm_rows(alpha * x_ref[...] + mix, g_ref[...], b_ref[...])
    o_ref[...] = out
    obf_ref[...] = out.astype(BF16)


def _matmul_residual_ln(y, w, x, g, b, *, alpha, tm):
    t, k = y.shape
    d = w.shape[1]
    row = pl.BlockSpec((tm, d), lambda i: (i, 0))
    vec = pl.BlockSpec((1, d), lambda i: (0, 0))
    return pl.pallas_call(
        functools.partial(_mm_res_ln_kernel, alpha=alpha),
        grid=(t // tm,),
        in_specs=[pl.BlockSpec((tm, k), lambda i: (i, 0)),
                  pl.BlockSpec((k, d), lambda i: (0, 0)),
                  row, vec, vec],
        out_specs=[row, row],
        out_shape=[jax.ShapeDtypeStruct((t, d), F32), jax.ShapeDtypeStruct((t, d), BF16)],
        compiler_params=_params(("parallel",)),
        name="matmul_residual_ln",
    )(y, w, x, g.reshape(1, d), b.reshape(1, d))


def _retention_kernel(q_ref, k_ref, v_ref, g_ref, dmask_ref, qdec_ref, kdec_ref, cdec_ref,
                      o_ref, state_ref, *, n_chunks):
    c = RET_CHUNK
    heads = state_ref.shape[0]
    dk = q_ref.shape[1] // heads
    dv = v_ref.shape[1] // heads
    state_ref[...] = jnp.zeros_like(state_ref)

    def chunk(ci, carry):
        rows = pl.ds(pl.multiple_of(ci * c, c), c)
        for p in range(heads):
            kcols = slice(p * dk, (p + 1) * dk)
            vcols = slice(p * dv, (p + 1) * dv)
            qi = q_ref[rows, kcols]
            ki = k_ref[rows, kcols]
            vi = v_ref[rows, vcols]
            state = state_ref[p]
            sc = _dot_nt(qi, ki) * dmask_ref[p]
            intra = _dot(sc.astype(BF16), vi)
            inter = _dot(qi, state.astype(BF16)) * qdec_ref[p]
            kd = (ki.astype(F32) * kdec_ref[p]).astype(BF16)
            state_ref[p] = state * cdec_ref[p] + _dot_tn(kd, vi)
            y = intra + inter
            mu = jnp.mean(y, axis=-1, keepdims=True)
            yc = y - mu
            var = jnp.mean(yc * yc, axis=-1, keepdims=True)
            yn = yc * lax.rsqrt(var + LN_EPS)
            gate = g_ref[rows, vcols].astype(F32)
            gate = gate * jax.nn.sigmoid(gate)
            o_ref[rows, vcols] = (gate * yn).astype(o_ref.dtype)
        return carry

    lax.fori_loop(0, n_chunks, chunk, 0)


def _retention_core(qk, vg, *, batch, seq, heads):
    h = RET_HEADS
    dk = qk.shape[1] // (2 * h)
    dv = vg.shape[1] // (2 * h)
    c = RET_CHUNK
    steps = h // heads
    log_g = jnp.log(1.0 - jnp.exp2(-5.0 - jnp.arange(h, dtype=F32)))
    ar = jnp.arange(c, dtype=F32)
    rel = ar[:, None] - ar[None, :]
    dmask = jnp.where(rel[None] >= 0, jnp.exp(jnp.maximum(rel, 0.0)[None] * log_g[:, None, None]), 0.0)
    qdec = jnp.broadcast_to(jnp.exp((ar + 1.0)[None] * log_g[:, None])[:, :, None], (h, c, dv))
    kdec = jnp.broadcast_to(jnp.exp((c - 1.0 - ar)[None] * log_g[:, None])[:, :, None], (h, c, dk))
    cdec = jnp.broadcast_to(jnp.exp(c * log_g)[:, None, None], (h, 1, dv))
    head_tab = lambda shape: pl.BlockSpec((heads,) + shape, lambda b, hh: (hh, 0, 0))
    return pl.pallas_call(
        functools.partial(_retention_kernel, n_chunks=seq // c),
        grid=(batch, steps),
        in_specs=[pl.BlockSpec((seq, heads * dk), lambda b, hh: (b, hh)),
                  pl.BlockSpec((seq, heads * dk), lambda b, hh: (b, steps + hh)),
                  pl.BlockSpec((seq, heads * dv), lambda b, hh: (b, hh)),
                  pl.BlockSpec((seq, heads * dv), lambda b, hh: (b, steps + hh)),
                  head_tab((c, c)), head_tab((c, dv)), head_tab((c, dk)), head_tab((1, dv))],
        out_specs=pl.BlockSpec((seq, heads * dv), lambda b, hh: (b, hh)),
        out_shape=jax.ShapeDtypeStruct((batch * seq, h * dv), BF16),
        scratch_shapes=[pltpu.VMEM((heads, dk, dv), F32)],
        compiler_params=_params(("parallel", "arbitrary")),
        name="retention_core",
    )(qk, qk, vg, vg, dmask, qdec, kdec, cdec)


def _diff_attn_kernel(q_ref, k_ref, v_ref, lam_ref, g_ref, o_ref, m_ref, l_ref, acc_ref,
                      *, tq, tk, lam_init, heads):
    i = pl.program_id(2)
    dv = q_ref.shape[1] // heads
    dh = dv // 2
    m_ref[...] = jnp.full_like(m_ref, NEG_BIG)
    l_ref[...] = jnp.zeros_like(l_ref)
    acc_ref[...] = jnp.zeros_like(acc_ref)
    q = q_ref[...]

    def kv_step(col_start, ncols, row_lo, masked):
        nr = tq - row_lo
        cols = pl.ds(pl.multiple_of(col_start, ncols), ncols)
        kj = k_ref[cols, :]
        vj = v_ref[cols, :]
        if masked:
            local_col = (col_start - i * tq) + lax.broadcasted_iota(jnp.int32, (nr, ncols), 1)
            visible = local_col <= row_lo + lax.broadcasted_iota(jnp.int32, (nr, ncols), 0)

        def scores(h):
            return [_dot_nt(q[row_lo:, h * dv + c * dh:h * dv + (c + 1) * dh],
                            kj[:, h * dv + c * dh:h * dv + (c + 1) * dh]) for c in range(2)]

        nxt = scores(0)
        for h in range(heads):
            cur = nxt
            if h + 1 < heads:
                nxt = scores(h + 1)
            ps = []
            corrs = []
            for c in range(2):
                s = cur[c]
                if masked:
                    s = jnp.where(visible, s, NEG_BIG)
                m_prev = m_ref[2 * h + c, row_lo:, :]
                m_next = jnp.maximum(m_prev, jnp.max(s, axis=1, keepdims=True))
                corr = jnp.exp(m_prev - m_next)
                p = jnp.exp(s - jnp.concatenate([m_next] * (ncols // LANES), axis=1))
                fold = p[:, :LANES]
                for g in range(1, ncols // LANES):
                    fold = fold + p[:, g * LANES:(g + 1) * LANES]
                l_ref[2 * h + c, row_lo:, :] = l_ref[2 * h + c, row_lo:, :] * corr + fold
                m_ref[2 * h + c, row_lo:, :] = m_next
                ps.append(p)
                corrs.append(corr)
            pv = _dot(jnp.concatenate(ps, axis=0).astype(BF16), vj[:, h * dv:(h + 1) * dv])
            acc_ref[2 * h, row_lo:, :] = acc_ref[2 * h, row_lo:, :] * corrs[0] + pv[:nr]
            acc_ref[2 * h + 1, row_lo:, :] = acc_ref[2 * h + 1, row_lo:, :] * corrs[1] + pv[nr:]

    def off_diagonal(j, carry):
        kv_step(j * tq, tq, 0, False)
        return carry

    lax.fori_loop(0, i, off_diagonal, 0)
    for c in range(tq // tk):
        kv_step(i * tq + c * tk, tk, c * tk, True)

    lp = lam_ref[...]
    lam = (jnp.exp(jnp.sum(lp[0:1] * lp[1:2], axis=1, keepdims=True))
           - jnp.exp(jnp.sum(lp[2:3] * lp[3:4], axis=1, keepdims=True)) + lam_init)
    for h in range(heads):
        l1 = jnp.sum(l_ref[2 * h], axis=1, keepdims=True)
        l2 = jnp.sum(l_ref[2 * h + 1], axis=1, keepdims=True)
        o = acc_ref[2 * h] / l1 - lam * (acc_ref[2 * h + 1] / l2)
        o = o * lax.rsqrt(jnp.mean(o * o, axis=-1, keepdims=True) + LN_EPS)
        o_ref[:, h * dv:(h + 1) * dv] = (o * g_ref[...] * (1.0 - lam_init)).astype(o_ref.dtype)


def _diff_attention_core(q, k, v, lam_p, subln_g, *, batch, seq, lam_init, tq, tk, heads):
    dv = v.shape[1] // DIFF_HEADS
    q_tiles = seq // tq
    width = heads * dv
    stat = pltpu.VMEM((2 * heads, tq, dv), F32)
    return pl.pallas_call(
        functools.partial(_diff_attn_kernel, tq=tq, tk=tk, lam_init=lam_init, heads=heads),
        grid=(batch, DIFF_HEADS // heads, q_tiles),
        in_specs=[pl.BlockSpec((tq, width), lambda b, hh, i: (b * q_tiles + i, hh)),
                  pl.BlockSpec((seq, width), lambda b, hh, i: (b, hh)),
                  pl.BlockSpec((seq, width), lambda b, hh, i: (b, hh)),
                  pl.BlockSpec(lam_p.shape, lambda b, hh, i: (0, 0)),
                  pl.BlockSpec((1, dv), lambda b, hh, i: (0, 0))],
        out_specs=pl.BlockSpec((tq, width), lambda b, hh, i: (b * q_tiles + i, hh)),
        out_shape=jax.ShapeDtypeStruct(q.shape, BF16),
        scratch_shapes=[stat] * 3,
        compiler_params=_params(("parallel", "parallel", "arbitrary")),
        name="diff_attention_core",
    )(q, k, v, lam_p, subln_g.reshape(1, dv))


def _descending_top(s, count):
    tops = []
    cur = s
    for _ in range(count):
        m = jnp.max(cur, axis=0, keepdims=True)
        tops.append(m)
        cur = jnp.where(cur >= m, NEG_BIG, cur)
    return tops


SUBLANES = 8


def _oddeven_merge_sort_pairs(n):
    pairs = []
    p = 1
    while p < n:
        k = p
        while k >= 1:
            for j in range(k % p, n - k, 2 * k):
                for i in range(min(k, n - j - k)):
                    if (i + j) // (2 * p) == (i + j + k) // (2 * p):
                        pairs.append((i + j, i + j + k))
            k //= 2
        p *= 2
    return pairs


def _bitonic_merge_pairs(n):
    pairs = []
    k = n // 2
    while k >= 1:
        for i in range(n):
            if i & k == 0:
                pairs.append((i, i + k))
        k //= 2
    return pairs


def _sorted_top(s):
    n = s.shape[0] // SUBLANES
    v = [s[i * SUBLANES:(i + 1) * SUBLANES, :] for i in range(n)]

    def exchange(pairs):
        for i, j in pairs:
            hi = jnp.maximum(v[i], v[j])
            v[j] = jnp.minimum(v[i], v[j])
            v[i] = hi

    exchange(_oddeven_merge_sort_pairs(n))
    shift = SUBLANES // 2
    while shift >= 1:
        other = [pltpu.roll(v[n - 1 - i], SUBLANES - shift, axis=0) for i in range(n)]
        for i in range(n):
            v[i] = jnp.maximum(v[i], other[i])
        exchange(_bitonic_merge_pairs(n))
        shift //= 2
    return [v[i][0:1, :] for i in range(n)]


def _peer_route_kernel(x_ref, wq_ref, keys_ref, xt_ref, e2_ref, thr_ref, coef_ref):
    k = PEER_TOPK
    x = x_ref[...]
    xt_ref[0] = x.T
    qt = _dot_nt(wq_ref[...], x).astype(BF16)
    half = keys_ref.shape[2]
    for h in range(PEER_HEADS):
        s1 = _dot(keys_ref[2 * h], qt[(2 * h) * half:(2 * h + 1) * half, :])
        s2 = _dot(keys_ref[2 * h + 1], qt[(2 * h + 1) * half:(2 * h + 2) * half, :])
        a = _sorted_top(s1)
        b = _sorted_top(s2)
        assert len(a) == k and len(b) == k
        a_all = jnp.concatenate(a, axis=0)
        b_all = jnp.concatenate(b, axis=0)
        cands = [a[0] + b_all]
        cands += [a[i] + b_all[:k // 2] for i in range(1, k // 2)]
        cands += [a_all[k // 2:] + b[0]]
        cand = jnp.concatenate(cands, axis=0)
        tau = _descending_top(cand, k)[k - 1]
        top = a[0] + b[0]
        z = jnp.sum(jnp.where(cand >= tau, jnp.exp(cand - top), 0.0), axis=0, keepdims=True)
        e2_ref[0, h] = jnp.exp(s2 - b[0])
        thr_ref[0, h] = jnp.exp((tau - b[0]) - s1)
        coef_ref[0, h] = jnp.exp(s1 - a[0]) / z


def _peer_route(x_bf, wq_t, keys, *, tm):
    t, d = x_bf.shape
    nk = keys.shape[1]
    heads = PEER_HEADS
    tok = pl.BlockSpec((1, heads, nk, tm), lambda i: (i, 0, 0, 0))
    tok_f32 = jax.ShapeDtypeStruct((t // tm, heads, nk, tm), F32)
    return pl.pallas_call(
        _peer_route_kernel,
        grid=(t // tm,),
        in_specs=[pl.BlockSpec((tm, d), lambda i: (i, 0)),
                  pl.BlockSpec(wq_t.shape, lambda i: (0, 0)),
                  pl.BlockSpec(keys.shape, lambda i: (0, 0, 0))],
        out_specs=[pl.BlockSpec((1, d, tm), lambda i: (i, 0, 0)), tok, tok, tok],
        out_shape=[jax.ShapeDtypeStruct((t // tm, d, tm), BF16), tok_f32, tok_f32, tok_f32],
        compiler_params=_params(("parallel",)),
        name="peer_route",
    )(x_bf, wq_t, keys)


def _gelu_tanh(x):
    inner = x * (0.7978845608028654 + 0.035677408136300125 * (x * x))
    hx = 0.5 * x
    return hx + hx * jnp.tanh(inner)


HALF_ROWS = 32


def _peer_expert_kernel(xt_ref, u_ref, vt_ref, e2_ref, thr_ref, coef_ref, x_ref, g_ref, b_ref,
                        o_ref, obf_ref, acc_ref, pt_ref, *ht_refs, alpha, sub, group):
    j = pl.program_id(1)
    nk = e2_ref.shape[2]
    tm = xt_ref.shape[2]
    te = u_ref.shape[0]
    n_sub = te // sub
    a_per_sub = sub // nk
    ahead = 2

    @pl.when(j == 0)
    def _():
        acc_ref[...] = jnp.zeros_like(acc_ref)

    def first_matmul(sb):
        ht_refs[sb][...] = _dot(u_ref[sb * sub:(sb + 1) * sub, :], xt_ref[0])

    for sb in range(min(ahead, n_sub)):
        first_matmul(sb)
    for sb in range(n_sub):
        ht_ref = ht_refs[sb]
        if sb + ahead < n_sub:
            first_matmul(sb + ahead)
        for lc in range(tm // LANES):
            lanes = slice(lc * LANES, (lc + 1) * LANES)
            for hb in range(nk // HALF_ROWS):
                keys = slice(hb * HALF_ROWS, (hb + 1) * HALF_ROWS)
                ws = [jnp.zeros((HALF_ROWS, LANES), F32)] * a_per_sub
                for h in range(PEER_HEADS):
                    e2 = e2_ref[0, h, keys, lanes]
                    for al in range(a_per_sub):
                        a = sb * a_per_sub + al
                        thr = thr_ref[0, h, a:a + 1, lanes]
                        coef = coef_ref[0, h, a:a + 1, lanes]
                        ws[al] = ws[al] + jnp.where(e2 >= thr, e2, 0.0) * coef
                for al in range(a_per_sub):
                    rows = slice(al * nk + hb * HALF_ROWS, al * nk + (hb + 1) * HALF_ROWS)
                    gated = (ws[al] * _gelu_tanh(ht_ref[rows, lanes])).astype(BF16)
                    pt_ref[sb * sub + rows.start:sb * sub + rows.stop, lanes] = gated
        if (sb + 1) % group == 0:
            grp = slice((sb + 1 - group) * sub, (sb + 1) * sub)
            acc_ref[...] += _dot(vt_ref[0, :, grp], pt_ref[grp, :])

    @pl.when(j == pl.num_programs(1) - 1)
    def _():
        out = _layer_norm_rows(alpha * x_ref[...] + acc_ref[...].T, g_ref[...], b_ref[...])
        o_ref[...] = out
        obf_ref[...] = out.astype(BF16)


def _peer_experts(xt, u, v, e2, thr, coef, x, g, b, *, alpha, te, sub, group):
    n_tiles, d, tm = xt.shape
    t = n_tiles * tm
    n_exp = u.shape[0]
    _, heads, nk, _ = e2.shape
    vt = v.reshape(n_exp // te, te, d).transpose(0, 2, 1)
    row = pl.BlockSpec((tm, d), lambda i, j: (i, 0))
    vec = pl.BlockSpec((1, d), lambda i, j: (0, 0))
    tok = pl.BlockSpec((1, heads, nk, tm), lambda i, j: (i, 0, 0, 0))
    blk = pl.BlockSpec((1, heads, te // nk, tm), lambda i, j: (i, 0, j, 0))
    return pl.pallas_call(
        functools.partial(_peer_expert_kernel, alpha=alpha, sub=sub, group=group),
        grid=(t // tm, n_exp // te),
        in_specs=[pl.BlockSpec((1, d, tm), lambda i, j: (i, 0, 0)),
                  pl.BlockSpec((te, d), lambda i, j: (j, 0)),
                  pl.BlockSpec((1, d, te), lambda i, j: (j, 0, 0)),
                  tok, blk, blk, row, vec, vec],
        out_specs=[row, row],
        out_shape=[jax.ShapeDtypeStruct((t, d), F32), jax.ShapeDtypeStruct((t, d), BF16)],
        scratch_shapes=([pltpu.VMEM((d, tm), F32), pltpu.VMEM((te, tm), BF16)]
                        + [pltpu.VMEM((sub, tm), F32)] * (te // sub)),
        compiler_params=_params(("parallel", "arbitrary")),
        name="peer_experts",
    )(xt, u, vt, e2, thr, coef, x, g.reshape(1, d), b.reshape(1, d))


def _tiles(batch, seq):
    t = batch * seq
    return dict(
        mm_tm=min(seq, 2048), mm_tn=1024,
        ln_tm=min(t, 1024), ret_heads=2,
        attn_tq=min(seq, 1024), attn_tk=min(seq, 512), attn_heads=2,
        peer_tm=min(t, 512), exp_te=2048, exp_sub=512, exp_group=2,
    )


def _rope_tables_full(seq, dim, theta):
    pos = jnp.arange(seq, dtype=F32)
    freqs = 1.0 / (theta ** jnp.linspace(0.0, 1.0, dim // 2, dtype=F32))
    ang = pos[:, None] * freqs[None, :]
    return jnp.cos(ang), jnp.sin(ang)


def _rope_tables_partial(seq, head_dim, rope_dim, theta):
    pos = jnp.arange(seq, dtype=F32)
    freqs = theta ** (-jnp.arange(0, rope_dim, 2, dtype=F32) / rope_dim)
    ang = pos[:, None] * freqs[None, :]
    r2 = rope_dim // 2
    lane = jnp.arange(LANES) % head_dim
    cos = jnp.take(jnp.cos(ang), lane % r2, axis=1)
    sin = jnp.take(jnp.sin(ang), lane % r2, axis=1)
    first = (lane < r2)[None, :]
    second = ((lane >= r2) & (lane < rope_dim))[None, :]
    c = jnp.where(first | second, cos, 1.0)
    s_next = jnp.where(first, -sin, 0.0)
    s_prev = jnp.where(second, sin, 0.0)
    return c, s_next, s_prev, r2


def kernel(x, ret_w_in, ret_w_out, kv_w, diff_w_q, diff_lambda, diff_subln_g, diff_w_out,
           peer_w_q, peer_subkeys, peer_u, peer_v, ln_g, ln_b):
    batch, seq, d = x.shape
    t = batch * seq
    depth = peer_w_q.shape[0]
    n_a = ret_w_in.shape[0]
    alpha = (2 * depth) ** 0.25
    tl = _tiles(batch, seq)

    ret_dk = d // RET_HEADS
    hq = RET_HEADS * ret_dk
    diff_dh = d // (2 * DIFF_HEADS)
    kw = DIFF_HEADS * 2 * diff_dh
    ret_cos, ret_sin = _rope_tables_full(seq, ret_dk, RET_THETA)
    dc, ds_next, ds_prev, rot = _rope_tables_partial(seq, diff_dh, diff_dh // 4, ROPE_THETA)

    xf = x.reshape(t, d)
    xb = xf.astype(BF16)
    k_sh = v_sh = None
    for l in range(depth):
        if l < n_a:
            w_in = ret_w_in[l].astype(BF16)
            qk = _matmul_rope_full(xb, w_in[:, :2 * hq], ret_cos, ret_sin, seq=seq, head_dim=ret_dk,
                                   n_unscaled_cols=hq, scale=ret_dk ** -0.5,
                                   tm=tl["mm_tm"], tn=tl["mm_tn"])
            vg = _matmul(xb, w_in[:, 2 * hq:], tm=tl["mm_tm"], tn=tl["mm_tn"])
            y = _retention_core(qk, vg, batch=batch, seq=seq, heads=tl["ret_heads"])
            w_out = ret_w_out[l].astype(BF16)
        else:
            j = l - n_a
            q = _matmul_rope_partial(xb, diff_w_q[j].astype(BF16), (dc, ds_next, ds_prev), seq=seq, rot=rot,
                                     scale=diff_dh ** -0.5, tm=tl["mm_tm"], tn=tl["mm_tn"])
            lam_init = 0.8 - 0.6 * math.exp(-0.3 * l)
            y = _diff_attention_core(q, k_sh, v_sh, diff_lambda[j], diff_subln_g[j], batch=batch, seq=seq,
                                     lam_init=lam_init, tq=tl["attn_tq"], tk=tl["attn_tk"], heads=tl["attn_heads"])
            w_out = diff_w_out[j].astype(BF16)
        xf, xb = _matmul_residual_ln(y, w_out, xf, ln_g[l, 0], ln_b[l, 0], alpha=alpha, tm=tl["ln_tm"])

        n_keys = peer_subkeys.shape[3]
        keys = peer_subkeys[l].reshape(2 * PEER_HEADS, n_keys, -1).astype(BF16)
        xt, e2, thr, coef = _peer_route(xb, peer_w_q[l].T.astype(BF16), keys, tm=tl["peer_tm"])
        xf, xb = _peer_experts(xt, peer_u[l].astype(BF16), peer_v[l].astype(BF16), e2, thr, coef,
                               xf, ln_g[l, 1], ln_b[l, 1], alpha=alpha,
                               te=tl["exp_te"], sub=tl["exp_sub"], group=tl["exp_group"])
        if l == n_a - 1:
            kv_wb = kv_w.astype(BF16)
            k_sh = _matmul_rope_partial(xb, kv_wb[:, :kw], (dc, ds_next, ds_prev), seq=seq, rot=rot,
                                        scale=1.0, tm=tl["mm_tm"], tn=tl["mm_tn"])
            v_sh = _matmul(xb, kv_wb[:, kw:], tm=tl["mm_tm"], tn=tl["mm_tn"])
    return xf.reshape(batch, seq, d)
```

```python
import functools
import math

import jax
import jax.numpy as jnp
from jax import lax
from jax.experimental import pallas as pl
from jax.experimental.pallas import tpu as pltpu

RET_HEADS = 4
RET_CHUNK = 128
RET_THETA = 10000.0
DIFF_HEADS = 8
ROPE_THETA = 500000.0
PEER_HEADS = 8
PEER_N_KEYS = 128
PEER_TOPK = 16
LN_EPS = 1e-5

LANES = 128
VMEM_LIMIT = 56 * 1024 * 1024
NEG_BIG = -1e30

F32 = jnp.float32
BF16 = jnp.bfloat16


def _params(sem, vmem=VMEM_LIMIT, flags=None):
    return pltpu.CompilerParams(dimension_semantics=sem, vmem_limit_bytes=vmem, flags=flags)


def _dot(a, b):
    return jnp.dot(a, b, preferred_element_type=F32)


def _dot_nt(a, b):
    return lax.dot_general(a, b, (((1,), (1,)), ((), ())), preferred_element_type=F32)


def _dot_tn(a, b):
    return lax.dot_general(a, b, (((0,), (0,)), ((), ())), preferred_element_type=F32)


def _mm_kernel(x_ref, w_ref, o_ref):
    o_ref[...] = _dot(x_ref[...], w_ref[...]).astype(o_ref.dtype)


def _matmul(x, w, *, tm, tn, out_dtype=BF16):
    t, k = x.shape
    n = w.shape[1]
    return pl.pallas_call(
        _mm_kernel,
        grid=(t // tm, n // tn),
        in_specs=[pl.BlockSpec((tm, k), lambda i, j: (i, 0)),
                  pl.BlockSpec((k, tn), lambda i, j: (0, j))],
        out_specs=pl.BlockSpec((tm, tn), lambda i, j: (i, j)),
        out_shape=jax.ShapeDtypeStruct((t, n), out_dtype),
        compiler_params=_params(("parallel", "arbitrary")),
        name="matmul",
    )(x, w)


def _mm_rope_full_kernel(x_ref, w_ref, cos_ref, sin_ref, o_ref, *, head_dim, n_unscaled_tiles, scale):
    acc = _dot(x_ref[...], w_ref[...])
    cos = cos_ref[...]
    sin = sin_ref[...]
    half = head_dim // 2
    s = jnp.where(pl.program_id(1) >= n_unscaled_tiles, scale, 1.0).astype(F32)
    for h in range(acc.shape[1] // head_dim):
        x1 = acc[:, h * head_dim:h * head_dim + half]
        x2 = acc[:, h * head_dim + half:(h + 1) * head_dim]
        o_ref[:, h * head_dim:h * head_dim + half] = ((x1 * cos - x2 * sin) * s).astype(o_ref.dtype)
        o_ref[:, h * head_dim + half:(h + 1) * head_dim] = ((x1 * sin + x2 * cos) * s).astype(o_ref.dtype)


def _matmul_rope_full(x, w, cos, sin, *, seq, head_dim, n_unscaled_cols, scale, tm, tn):
    t, k = x.shape
    n = w.shape[1]
    pos_tiles = seq // tm
    kern = functools.partial(_mm_rope_full_kernel, head_dim=head_dim,
                             n_unscaled_tiles=n_unscaled_cols // tn, scale=scale)
    return pl.pallas_call(
        kern,
        grid=(t // tm, n // tn),
        in_specs=[pl.BlockSpec((tm, k), lambda i, j: (i, 0)),
                  pl.BlockSpec((k, tn), lambda i, j: (0, j)),
                  pl.BlockSpec((tm, head_dim // 2), lambda i, j: (i % pos_tiles, 0)),
                  pl.BlockSpec((tm, head_dim // 2), lambda i, j: (i % pos_tiles, 0))],
        out_specs=pl.BlockSpec((tm, tn), lambda i, j: (i, j)),
        out_shape=jax.ShapeDtypeStruct((t, n), BF16),
        compiler_params=_params(("parallel", "arbitrary")),
        name="matmul_rope_full",
    )(x, w, cos, sin)


def _mm_rope_partial_kernel(x_ref, w_ref, c_ref, s_next_ref, s_prev_ref, o_ref, *, rot, scale):
    acc = _dot(x_ref[...], w_ref[...])
    c = c_ref[...]
    s_next = s_next_ref[...]
    s_prev = s_prev_ref[...]
    for g in range(acc.shape[1] // LANES):
        xg = acc[:, g * LANES:(g + 1) * LANES]
        nxt = pltpu.roll(xg, LANES - rot, axis=1)
        prv = pltpu.roll(xg, rot, axis=1)
        o_ref[:, g * LANES:(g + 1) * LANES] = ((xg * c + nxt * s_next + prv * s_prev) * scale).astype(o_ref.dtype)


def _matmul_rope_partial(x, w, tables, *, seq, rot, scale, tm, tn):
    t, k = x.shape
    n = w.shape[1]
    pos_tiles = seq // tm
    kern = functools.partial(_mm_rope_partial_kernel, rot=rot, scale=scale)
    tab_spec = pl.BlockSpec((tm, LANES), lambda i, j: (i % pos_tiles, 0))
    return pl.pallas_call(
        kern,
        grid=(t // tm, n // tn),
        in_specs=[pl.BlockSpec((tm, k), lambda i, j: (i, 0)),
                  pl.BlockSpec((k, tn), lambda i, j: (0, j)),
                  tab_spec, tab_spec, tab_spec],
        out_specs=pl.BlockSpec((tm, tn), lambda i, j: (i, j)),
        out_shape=jax.ShapeDtypeStruct((t, n), BF16),
        compiler_params=_params(("parallel", "arbitrary")),
        name="matmul_rope_partial",
    )(x, w, *tables)


def _layer_norm_rows(z, g, b):
    mu = jnp.mean(z, axis=-1, keepdims=True)
    zc = z - mu
    var = jnp.mean(zc * zc, axis=-1, keepdims=True)
    return zc * lax.rsqrt(var + LN_EPS) * g + b


def _mm_res_ln_kernel(y_ref, w_ref, x_ref, g_ref, b_ref, o_ref, obf_ref, *, alpha):
    mix = _dot(y_ref[...], w_ref[...])
    out = _layer_norm_rows(alpha * x_ref[...] + mix, g_ref[...], b_ref[...])
    o_ref[...] = out
    obf_ref[...] = out.astype(BF16)


def _matmul_residual_ln(y, w, x, g, b, *, alpha, tm):
    t, k = y.shape
    d = w.shape[1]
    row = pl.BlockSpec((tm, d), lambda i: (i, 0))
    vec = pl.BlockSpec((1, d), lambda i: (0, 0))
    return pl.pallas_call(
        functools.partial(_mm_res_ln_kernel, alpha=alpha),
        grid=(t // tm,),
        in_specs=[pl.BlockSpec((tm, k), lambda i: (i, 0)),
                  pl.BlockSpec((k, d), lambda i: (0, 0)),
                  row, vec, vec],
        out_specs=[row, row],
        out_shape=[jax.ShapeDtypeStruct((t, d), F32), jax.ShapeDtypeStruct((t, d), BF16)],
        compiler_params=_params(("parallel",)),
        name="matmul_residual_ln",
    )(y, w, x, g.reshape(1, d), b.reshape(1, d))


def _retention_kernel(q_ref, k_ref, v_ref, g_ref, dmask_ref, qdec_ref, kdec_ref, cdec_ref,
                      o_ref, state_ref, *, n_chunks):
    c = RET_CHUNK
    heads = state_ref.shape[0]
    dk = q_ref.shape[1] // heads
    dv = v_ref.shape[1] // heads
    state_ref[...] = jnp.zeros_like(state_ref)

    def chunk(ci, carry):
        rows = pl.ds(pl.multiple_of(ci * c, c), c)
        for p in range(heads):
            kcols = slice(p * dk, (p + 1) * dk)
            vcols = slice(p * dv, (p + 1) * dv)
            qi = q_ref[rows, kcols]
            ki = k_ref[rows, kcols]
            vi = v_ref[rows, vcols]
            state = state_ref[p]
            sc = _dot_nt(qi, ki) * dmask_ref[p]
            intra = _dot(sc.astype(BF16), vi)
            inter = _dot(qi, state.astype(BF16)) * qdec_ref[p]
            kd = (ki.astype(F32) * kdec_ref[p]).astype(BF16)
            state_ref[p] = state * cdec_ref[p] + _dot_tn(kd, vi)
            y = intra + inter
            mu = jnp.mean(y, axis=-1, keepdims=True)
            yc = y - mu
            var = jnp.mean(yc * yc, axis=-1, keepdims=True)
            yn = yc * lax.rsqrt(var + LN_EPS)
            gate = g_ref[rows, vcols].astype(F32)
            gate = gate * jax.nn.sigmoid(gate)
            o_ref[rows, vcols] = (gate * yn).astype(o_ref.dtype)
        return carry

    lax.fori_loop(0, n_chunks, chunk, 0)


def _retention_core(qk, vg, *, batch, seq, heads):
    h = RET_HEADS
    dk = qk.shape[1] // (2 * h)
    dv = vg.shape[1] // (2 * h)
    c = RET_CHUNK
    steps = h // heads
    log_g = jnp.log(1.0 - jnp.exp2(-5.0 - jnp.arange(h, dtype=F32)))
    ar = jnp.arange(c, dtype=F32)
    rel = ar[:, None] - ar[None, :]
    dmask = jnp.where(rel[None] >= 0, jnp.exp(jnp.maximum(rel, 0.0)[None] * log_g[:, None, None]), 0.0)
    qdec = jnp.broadcast_to(jnp.exp((ar + 1.0)[None] * log_g[:, None])[:, :, None], (h, c, dv))
    kdec = jnp.broadcast_to(jnp.exp((c - 1.0 - ar)[None] * log_g[:, None])[:, :, None], (h, c, dk))
    cdec = jnp.broadcast_to(jnp.exp(c * log_g)[:, None, None], (h, 1, dv))
    head_tab = lambda shape: pl.BlockSpec((heads,) + shape, lambda b, hh: (hh, 0, 0))
    return pl.pallas_call(
        functools.partial(_retention_kernel, n_chunks=seq // c),
        grid=(batch, steps),
        in_specs=[pl.BlockSpec((seq, heads * dk), lambda b, hh: (b, hh)),
                  pl.BlockSpec((seq, heads * dk), lambda b, hh: (b, steps + hh)),
                  pl.BlockSpec((seq, heads * dv), lambda b, hh: (b, hh)),
                  pl.BlockSpec((seq, heads * dv), lambda b, hh: (b, steps + hh)),
                  head_tab((c, c)), head_tab((c, dv)), head_tab((c, dk)), head_tab((1, dv))],
        out_specs=pl.BlockSpec((seq, heads * dv), lambda b, hh: (b, hh)),
        out_shape=jax.ShapeDtypeStruct((batch * seq, h * dv), BF16),
        scratch_shapes=[pltpu.VMEM((heads, dk, dv), F32)],
        compiler_params=_params(("parallel", "arbitrary")),
        name="retention_core",
    )(qk, qk, vg, vg, dmask, qdec, kdec, cdec)


def _diff_attn_kernel(q_ref, k_ref, v_ref, lam_ref, g_ref, o_ref, m_ref, l_ref, acc_ref,
                      *, tq, tk, lam_init, heads):
    i = pl.program_id(2)
    dv = q_ref.shape[1] // heads
    dh = dv // 2
    m_ref[...] = jnp.full_like(m_ref, NEG_BIG)
    l_ref[...] = jnp.zeros_like(l_ref)
    acc_ref[...] = jnp.zeros_like(acc_ref)
    q = q_ref[...]

    def kv_step(col_start, ncols, row_lo, masked):
        nr = tq - row_lo
        cols = pl.ds(pl.multiple_of(col_start, ncols), ncols)
        kj = k_ref[cols, :]
        vj = v_ref[cols, :]
        if masked:
            local_col = (col_start - i * tq) + lax.broadcasted_iota(jnp.int32, (nr, ncols), 1)
            visible = local_col <= row_lo + lax.broadcasted_iota(jnp.int32, (nr, ncols), 0)

        def scores(h):
            return [_dot_nt(q[row_lo:, h * dv + c * dh:h * dv + (c + 1) * dh],
                            kj[:, h * dv + c * dh:h * dv + (c + 1) * dh]) for c in range(2)]

        nxt = scores(0)
        for h in range(heads):
            cur = nxt
            if h + 1 < heads:
                nxt = scores(h + 1)
            ps = []
            corrs = []
            for c in range(2):
                s = cur[c]
                if masked:
                    s = jnp.where(visible, s, NEG_BIG)
                m_prev = m_ref[2 * h + c, row_lo:, :]
                m_next = jnp.maximum(m_prev, jnp.max(s, axis=1, keepdims=True))
                corr = jnp.exp(m_prev - m_next)
                p = jnp.exp(s - jnp.concatenate([m_next] * (ncols // LANES), axis=1))
                fold = p[:, :LANES]
                for g in range(1, ncols // LANES):
                    fold = fold + p[:, g * LANES:(g + 1) * LANES]
                l_ref[2 * h + c, row_lo:, :] = l_ref[2 * h + c, row_lo:, :] * corr + fold
                m_ref[2 * h + c, row_lo:, :] = m_next
                ps.append(p)
                corrs.append(corr)
            pv = _dot(jnp.concatenate(ps, axis=0).astype(BF16), vj[:, h * dv:(h + 1) * dv])
            acc_ref[2 * h, row_lo:, :] = acc_ref[2 * h, row_lo:, :] * corrs[0] + pv[:nr]
            acc_ref[2 * h + 1, row_lo:, :] = acc_ref[2 * h + 1, row_lo:, :] * corrs[1] + pv[nr:]

    def off_diagonal(j, carry):
        kv_step(j * tq, tq, 0, False)
        return carry

    lax.fori_loop(0, i, off_diagonal, 0)
    for c in range(tq // tk):
        kv_step(i * tq + c * tk, tk, c * tk, True)

    lp = lam_ref[...]
    lam = (jnp.exp(jnp.sum(lp[0:1] * lp[1:2], axis=1, keepdims=True))
           - jnp.exp(jnp.sum(lp[2:3] * lp[3:4], axis=1, keepdims=True)) + lam_init)
    for h in range(heads):
        l1 = jnp.sum(l_ref[2 * h], axis=1, keepdims=True)
        l2 = jnp.sum(l_ref[2 * h + 1], axis=1, keepdims=True)
        o = acc_ref[2 * h] / l1 - lam * (acc_ref[2 * h + 1] / l2)
        o = o * lax.rsqrt(jnp.mean(o * o, axis=-1, keepdims=True) + LN_EPS)
        o_ref[:, h * dv:(h + 1) * dv] = (o * g_ref[...] * (1.0 - lam_init)).astype(o_ref.dtype)


def _diff_attention_core(q, k, v, lam_p, subln_g, *, batch, seq, lam_init, tq, tk, heads):
    dv = v.shape[1] // DIFF_HEADS
    q_tiles = seq // tq
    width = heads * dv
    stat = pltpu.VMEM((2 * heads, tq, dv), F32)
    return pl.pallas_call(
        functools.partial(_diff_attn_kernel, tq=tq, tk=tk, lam_init=lam_init, heads=heads),
        grid=(batch, DIFF_HEADS // heads, q_tiles),
        in_specs=[pl.BlockSpec((tq, width), lambda b, hh, i: (b * q_tiles + i, hh)),
                  pl.BlockSpec((seq, width), lambda b, hh, i: (b, hh)),
                  pl.BlockSpec((seq, width), lambda b, hh, i: (b, hh)),
                  pl.BlockSpec(lam_p.shape, lambda b, hh, i: (0, 0)),
                  pl.BlockSpec((1, dv), lambda b, hh, i: (0, 0))],
        out_specs=pl.BlockSpec((tq, width), lambda b, hh, i: (b * q_tiles + i, hh)),
        out_shape=jax.ShapeDtypeStruct(q.shape, BF16),
        scratch_shapes=[stat] * 3,
        compiler_params=_params(("parallel", "parallel", "arbitrary")),
        name="diff_attention_core",
    )(q, k, v, lam_p, subln_g.reshape(1, dv))


def _descending_top(s, count):
    tops = []
    cur = s
    for _ in range(count):
        m = jnp.max(cur, axis=0, keepdims=True)
        tops.append(m)
        cur = jnp.where(cur >= m, NEG_BIG, cur)
    return tops


SUBLANES = 8


def _oddeven_merge_sort_pairs(n):
    pairs = []
    p = 1
    while p < n:
        k = p
        while k >= 1:
            for j in range(k % p, n - k, 2 * k):
                for i in range(min(k, n - j - k)):
                    if (i + j) // (2 * p) == (i + j + k) // (2 * p):
                        pairs.append((i + j, i + j + k))
            k //= 2
        p *= 2
    return pairs


def _bitonic_merge_pairs(n):
    pairs = []
    k = n // 2
    while k >= 1:
        for i in range(n):
            if i & k == 0:
                pairs.append((i, i + k))
        k //= 2
    return pairs


def _sorted_top(s):
    n = s.shape[0] // SUBLANES
    v = [s[i * SUBLANES:(i + 1) * SUBLANES, :] for i in range(n)]

    def exchange(pairs):
        for i, j in pairs:
            hi = jnp.maximum(v[i], v[j])
            v[j] = jnp.minimum(v[i], v[j])
            v[i] = hi

    exchange(_oddeven_merge_sort_pairs(n))
    shift = SUBLANES // 2
    while shift >= 1:
        other = [pltpu.roll(v[n - 1 - i], SUBLANES - shift, axis=0) for i in range(n)]
        for i in range(n):
            v[i] = jnp.maximum(v[i], other[i])
        exchange(_bitonic_merge_pairs(n))
        shift //= 2
    return [v[i][0:1, :] for i in range(n)]


def _peer_route_kernel(x_ref, wq_ref, keys_ref, xt_ref, e2_ref, thr_ref, coef_ref):
    k = PEER_TOPK
    x = x_ref[...]
    xt_ref[0] = x.T
    qt = _dot_nt(wq_ref[...], x).astype(BF16)
    half = keys_ref.shape[2]
    for h in range(PEER_HEADS):
        s1 = _dot(keys_ref[2 * h], qt[(2 * h) * half:(2 * h + 1) * half, :])
        s2 = _dot(keys_ref[2 * h + 1], qt[(2 * h + 1) * half:(2 * h + 2) * half, :])
        a = _sorted_top(s1)
        b = _sorted_top(s2)
        assert len(a) == k and len(b) == k
        a_all = jnp.concatenate(a, axis=0)
        b_all = jnp.concatenate(b, axis=0)
        cands = [a[0] + b_all]
        cands += [a[i] + b_all[:k // 2] for i in range(1, k // 2)]
        cands += [a_all[k // 2:] + b[0]]
        cand = jnp.concatenate(cands, axis=0)
        tau = _descending_top(cand, k)[k - 1]
        top = a[0] + b[0]
        z = jnp.sum(jnp.where(cand >= tau, jnp.exp(cand - top), 0.0), axis=0, keepdims=True)
        e2_ref[0, h] = jnp.exp(s2 - b[0])
        thr_ref[0, h] = jnp.exp((tau - b[0]) - s1)
        coef_ref[0, h] = jnp.exp(s1 - a[0]) / z


def _peer_route(x_bf, wq_t, keys, *, tm):
    t, d = x_bf.shape
    nk = keys.shape[1]
    heads = PEER_HEADS
    tok = pl.BlockSpec((1, heads, nk, tm), lambda i: (i, 0, 0, 0))
    tok_f32 = jax.ShapeDtypeStruct((t // tm, heads, nk, tm), F32)
    return pl.pallas_call(
        _peer_route_kernel,
        grid=(t // tm,),
        in_specs=[pl.BlockSpec((tm, d), lambda i: (i, 0)),
                  pl.BlockSpec(wq_t.shape, lambda i: (0, 0)),
                  pl.BlockSpec(keys.shape, lambda i: (0, 0, 0))],
        out_specs=[pl.BlockSpec((1, d, tm), lambda i: (i, 0, 0)), tok, tok, tok],
        out_shape=[jax.ShapeDtypeStruct((t // tm, d, tm), BF16), tok_f32, tok_f32, tok_f32],
        compiler_params=_params(("parallel",)),
        name="peer_route",
    )(x_bf, wq_t, keys)


_LOG2_E = 1.4426950408889634
_GELU_K1 = -2.0 * _LOG2_E * 0.7978845608028654
_GELU_K2 = -2.0 * _LOG2_E * 0.035677408136300125


def _gelu_tanh(x):
    t = x * (_GELU_K1 + _GELU_K2 * (x * x))
    return x / (1.0 + jnp.exp2(t))


HALF_ROWS = 32


def _peer_expert_kernel(xt_ref, u_ref, vt_ref, e2_ref, thr_ref, coef_ref, x_ref, g_ref, b_ref,
                        o_ref, obf_ref, acc_ref, pt_ref, *ht_refs, alpha, sub, group):
    j = pl.program_id(1)
    nk = e2_ref.shape[2]
    tm = xt_ref.shape[2]
    te = u_ref.shape[0]
    n_sub = te // sub
    a_per_sub = sub // nk
    ahead = 2

    @pl.when(j == 0)
    def _():
        acc_ref[...] = jnp.zeros_like(acc_ref)

    def first_matmul(sb):
        ht_refs[sb][...] = _dot(u_ref[sb * sub:(sb + 1) * sub, :], xt_ref[0])

    for sb in range(min(ahead, n_sub)):
        first_matmul(sb)
    for sb in range(n_sub):
        ht_ref = ht_refs[sb]
        if sb + ahead < n_sub:
            first_matmul(sb + ahead)
        for lc in range(tm // LANES):
            lanes = slice(lc * LANES, (lc + 1) * LANES)
            for hb in range(nk // HALF_ROWS):
                keys = slice(hb * HALF_ROWS, (hb + 1) * HALF_ROWS)
                ws = [jnp.zeros((HALF_ROWS, LANES), F32)] * a_per_sub
                for h in range(PEER_HEADS):
                    e2 = e2_ref[0, h, keys, lanes]
                    for al in range(a_per_sub):
                        a = sb * a_per_sub + al
                        thr = thr_ref[0, h, a:a + 1, lanes]
                        coef = coef_ref[0, h, a:a + 1, lanes]
                        ws[al] = ws[al] + jnp.where(e2 >= thr, e2, 0.0) * coef
                for al in range(a_per_sub):
                    rows = slice(al * nk + hb * HALF_ROWS, al * nk + (hb + 1) * HALF_ROWS)
                    gated = (ws[al] * _gelu_tanh(ht_ref[rows, lanes])).astype(BF16)
                    pt_ref[sb * sub + rows.start:sb * sub + rows.stop, lanes] = gated
        if (sb + 1) % group == 0:
            grp = slice((sb + 1 - group) * sub, (sb + 1) * sub)
            acc_ref[...] += _dot(vt_ref[0, :, grp], pt_ref[grp, :])

    @pl.when(j == pl.num_programs(1) - 1)
    def _():
        out = _layer_norm_rows(alpha * x_ref[...] + acc_ref[...].T, g_ref[...], b_ref[...])
        o_ref[...] = out
        obf_ref[...] = out.astype(BF16)


def _peer_experts(xt, u, v, e2, thr, coef, x, g, b, *, alpha, te, sub, group):
    n_tiles, d, tm = xt.shape
    t = n_tiles * tm
    n_exp = u.shape[0]
    _, heads, nk, _ = e2.shape
    vt = v.reshape(n_exp // te, te, d).transpose(0, 2, 1)
    row = pl.BlockSpec((tm, d), lambda i, j: (i, 0))
    vec = pl.BlockSpec((1, d), lambda i, j: (0, 0))
    tok = pl.BlockSpec((1, heads, nk, tm), lambda i, j: (i, 0, 0, 0))
    blk = pl.BlockSpec((1, heads, te // nk, tm), lambda i, j: (i, 0, j, 0))
    return pl.pallas_call(
        functools.partial(_peer_expert_kernel, alpha=alpha, sub=sub, group=group),
        grid=(t // tm, n_exp // te),
        in_specs=[pl.BlockSpec((1, d, tm), lambda i, j: (i, 0, 0)),
                  pl.BlockSpec((te, d), lambda i, j: (j, 0)),
                  pl.BlockSpec((1, d, te), lambda i, j: (j, 0, 0)),
                  tok, blk, blk, row, vec, vec],
        out_specs=[row, row],
        out_shape=[jax.ShapeDtypeStruct((t, d), F32), jax.ShapeDtypeStruct((t, d), BF16)],
        scratch_shapes=([pltpu.VMEM((d, tm), F32), pltpu.VMEM((te, tm), BF16)]
                        + [pltpu.VMEM((sub, tm), F32)] * (te // sub)),
        compiler_params=_params(("parallel", "arbitrary")),
        name="peer_experts",
    )(xt, u, vt, e2, thr, coef, x, g.reshape(1, d), b.reshape(1, d))


def _tiles(batch, seq):
    t = batch * seq
    return dict(
        mm_tm=min(seq, 2048), mm_tn=1024,
        ln_tm=min(t, 1024), ret_heads=2,
        attn_tq=min(seq, 1024), attn_tk=min(seq, 512), attn_heads=2,
        peer_tm=min(t, 512), exp_te=2048, exp_sub=512, exp_group=2,
    )


def _rope_tables_full(seq, dim, theta):
    pos = jnp.arange(seq, dtype=F32)
    freqs = 1.0 / (theta ** jnp.linspace(0.0, 1.0, dim // 2, dtype=F32))
    ang = pos[:, None] * freqs[None, :]
    return jnp.cos(ang), jnp.sin(ang)


def _rope_tables_partial(seq, head_dim, rope_dim, theta):
    pos = jnp.arange(seq, dtype=F32)
    freqs = theta ** (-jnp.arange(0, rope_dim, 2, dtype=F32) / rope_dim)
    ang = pos[:, None] * freqs[None, :]
    r2 = rope_dim // 2
    lane = jnp.arange(LANES) % head_dim
    cos = jnp.take(jnp.cos(ang), lane % r2, axis=1)
    sin = jnp.take(jnp.sin(ang), lane % r2, axis=1)
    first = (lane < r2)[None, :]
    second = ((lane >= r2) & (lane < rope_dim))[None, :]
    c = jnp.where(first | second, cos, 1.0)
    s_next = jnp.where(first, -sin, 0.0)
    s_prev = jnp.where(second, sin, 0.0)
    return c, s_next, s_prev, r2


def kernel(x, ret_w_in, ret_w_out, kv_w, diff_w_q, diff_lambda, diff_subln_g, diff_w_out,
           peer_w_q, peer_subkeys, peer_u, peer_v, ln_g, ln_b):
    batch, seq, d = x.shape
    t = batch * seq
    depth = peer_w_q.shape[0]
    n_a = ret_w_in.shape[0]
    alpha = (2 * depth) ** 0.25
    tl = _tiles(batch, seq)

    ret_dk = d // RET_HEADS
    hq = RET_HEADS * ret_dk
    diff_dh = d // (2 * DIFF_HEADS)
    kw = DIFF_HEADS * 2 * diff_dh
    ret_cos, ret_sin = _rope_tables_full(seq, ret_dk, RET_THETA)
    dc, ds_next, ds_prev, rot = _rope_tables_partial(seq, diff_dh, diff_dh // 4, ROPE_THETA)

    xf = x.reshape(t, d)
    xb = xf.astype(BF16)
    k_sh = v_sh = None
    for l in range(depth):
        if l < n_a:
            w_in = ret_w_in[l].astype(BF16)
            qk = _matmul_rope_full(xb, w_in[:, :2 * hq], ret_cos, ret_sin, seq=seq, head_dim=ret_dk,
                                   n_unscaled_cols=hq, scale=ret_dk ** -0.5,
                                   tm=tl["mm_tm"], tn=tl["mm_tn"])
            vg = _matmul(xb, w_in[:, 2 * hq:], tm=tl["mm_tm"], tn=tl["mm_tn"])
            y = _retention_core(qk, vg, batch=batch, seq=seq, heads=tl["ret_heads"])
            w_out = ret_w_out[l].astype(BF16)
        else:
            j = l - n_a
            q = _matmul_rope_partial(xb, diff_w_q[j].astype(BF16), (dc, ds_next, ds_prev), seq=seq, rot=rot,
                                     scale=diff_dh ** -0.5, tm=tl["mm_tm"], tn=tl["mm_tn"])
            lam_init = 0.8 - 0.6 * math.exp(-0.3 * l)
            y = _diff_attention_core(q, k_sh, v_sh, diff_lambda[j], diff_subln_g[j], batch=batch, seq=seq,
                                     lam_init=lam_init, tq=tl["attn_tq"], tk=tl["attn_tk"], heads=tl["attn_heads"])
            w_out = diff_w_out[j].astype(BF16)
        xf, xb = _matmul_residual_ln(y, w_out, xf, ln_g[l, 0], ln_b[l, 0], alpha=alpha, tm=tl["ln_tm"])

        n_keys = peer_subkeys.shape[3]
        keys = peer_subkeys[l].reshape(2 * PEER_HEADS, n_keys, -1).astype(BF16)
        xt, e2, thr, coef = _peer_route(xb, peer_w_q[l].T.astype(BF16), keys, tm=tl["peer_tm"])
        xf, xb = _peer_experts(xt, peer_u[l].astype(BF16), peer_v[l].astype(BF16), e2, thr, coef,
                               xf, ln_g[l, 1], ln_b[l, 1], alpha=alpha,
                               te=tl["exp_te"], sub=tl["exp_sub"], group=tl["exp_group"])
        if l == n_a - 1:
            kv_wb = kv_w.astype(BF16)
            k_sh = _matmul_rope_partial(xb, kv_wb[:, :kw], (dc, ds_next, ds_prev), seq=seq, rot=rot,
                                        scale=1.0, tm=tl["mm_tm"], tn=tl["mm_tn"])
            v_sh = _matmul(xb, kv_wb[:, kw:], tm=tl["mm_tm"], tn=tl["mm_tn"])
    return xf.reshape(batch, seq, d)
```
